```python
import math
import jax, jax.numpy as jnp
from jax import lax
import numpy as np

D_MODEL = 2048
BATCH = 8
SEQ = 2048
DEPTH = 2
DEC_BATCH = 16
DEC_SEQ = 16
PAST_LEN = 2048

CHUNK = 64
QBLOCK = 128
PLE_DIM = 256
MIX_WIDTH = D_MODEL
ATTN_WIDTH = MIX_WIDTH // 2
SSM_WIDTH = MIX_WIDTH - ATTN_WIDTH
HEAD_DIM = 128
N_HEADS = ATTN_WIDTH // HEAD_DIM
SSM_GROUP = 16
N_SSM_GROUPS = SSM_WIDTH // SSM_GROUP
SSM_STATE = 64
D_FF = ((8 * D_MODEL + 3 * 256 - 1) // (3 * 256)) * 256
IN_COLS = 3 * ATTN_WIDTH + N_HEADS + SSM_WIDTH
EPS = 1e-6
NEG_INF = -1e30

kernel_name = 'hybrid_fox_s5_stream_step'


def rms_norm(x, g):
    xf = x.astype(jnp.float32)
    y = xf * lax.rsqrt(jnp.mean(xf * xf, axis=-1, keepdims=True) + EPS)
    return (y * g.astype(jnp.float32)).astype(x.dtype)


def fox_attend(q, k, v, cq, ck, pos_q, pos_k):
    s = jnp.einsum('bqhd,bkhd->bhqk', q, k, preferred_element_type=jnp.float32) * (HEAD_DIM ** -0.5)
    bias = cq.astype(jnp.float32).transpose(0, 2, 1)[..., :, None] - ck.astype(jnp.float32).transpose(0, 2, 1)[..., None, :]
    s = jnp.where(pos_k[None, :] <= pos_q[:, None], s + bias, NEG_INF)
    p = jax.nn.softmax(s, axis=-1)
    return jnp.einsum('bhqk,bkhd->bqhd', p.astype(v.dtype), v)


def fox_prompt(q, k, v, c):
    b, t, h, d = q.shape
    nb = t // QBLOCK
    pos = jnp.arange(t)
    qb = q.reshape(b, nb, QBLOCK, h, d).transpose(1, 0, 2, 3, 4)
    cb = c.reshape(b, nb, QBLOCK, h).transpose(1, 0, 2, 3)
    pb = pos.reshape(nb, QBLOCK)

    def one_block(args):
        qi, ci, pi = args
        return fox_attend(qi, k, v, ci, c, pi, pos)

    out = lax.map(one_block, (qb, cb, pb))
    return out.transpose(1, 0, 2, 3, 4).reshape(b, t, h * d)


def s5_mixer(u, x0_re, x0_im, a_re, a_im, log_dt, b_re, b_im, c_re, c_im, d_skip, w_glu, b_glu):
    f32 = jnp.float32
    bsz, t, _ = u.shape
    ug = u.astype(f32).reshape(bsz, t, N_SSM_GROUPS, SSM_GROUP)
    ar, ai = a_re.astype(f32), a_im.astype(f32)
    dt = jnp.exp(log_dt.astype(f32))[:, None]
    mag = jnp.exp(ar * dt)
    ang = ai * dt
    abar_re, abar_im = mag * jnp.cos(ang), mag * jnp.sin(ang)
    den = ar * ar + ai * ai
    nr, ni = abar_re - 1.0, abar_im
    coef_re = (nr * ar + ni * ai) / den
    coef_im = (ni * ar - nr * ai) / den
    br, bi = b_re.astype(f32), b_im.astype(f32)
    bbar_re = coef_re[..., None] * br - coef_im[..., None] * bi
    bbar_im = coef_re[..., None] * bi + coef_im[..., None] * br
    bu_re = jnp.einsum('gpc,btgc->btgp', bbar_re, ug)
    bu_im = jnp.einsum('gpc,btgc->btgp', bbar_im, ug)
    x0r, x0i = x0_re.astype(f32), x0_im.astype(f32)
    bu_re = bu_re.at[:, 0].add(abar_re * x0r - abar_im * x0i)
    bu_im = bu_im.at[:, 0].add(abar_re * x0i + abar_im * x0r)
    a_r = jnp.broadcast_to(abar_re, bu_re.shape)
    a_i = jnp.broadcast_to(abar_im, bu_re.shape)

    def combine(e1, e2):
        a1r, a1i, b1r, b1i = e1
        a2r, a2i, b2r, b2i = e2
        return (a2r * a1r - a2i * a1i,
                a2r * a1i + a2i * a1r,
                a2r * b1r - a2i * b1i + b2r,
                a2r * b1i + a2i * b1r + b2i)

    _, _, xr, xi = lax.associative_scan(combine, (a_r, a_i, bu_re, bu_im), axis=1)
    y = (jnp.einsum('gcp,btgp->btgc', c_re.astype(f32), xr)
         - jnp.einsum('gcp,btgp->btgc', c_im.astype(f32), xi)
         + d_skip.astype(f32).reshape(N_SSM_GROUPS, SSM_GROUP) * ug)
    g = jax.nn.gelu(y.reshape(bsz, t, SSM_WIDTH))
    out = g * jax.nn.sigmoid(g @ w_glu.astype(f32) + b_glu.astype(f32))
    return out.astype(u.dtype), xr[:, -1], xi[:, -1]


def trunk_layer(x, p_i, k_past, v_past, logf_past, s_re, s_im,
                g_mix, w_in, b_f, g_q, g_k, a_re, a_im, log_dt, b_re, b_im, c_re, c_im,
                d_skip, w_glu, b_glu, g_attn_out, g_ssm_out, w_out, g_ffn, w_gate, w_up,
                w_down, g_ple, w_ple_gate, w_ple_proj):
    bsz, t, _ = x.shape
    a = rms_norm(x, g_mix)
    proj = a @ w_in
    q = proj[..., :ATTN_WIDTH].reshape(bsz, t, N_HEADS, HEAD_DIM)
    k = proj[..., ATTN_WIDTH:2 * ATTN_WIDTH].reshape(bsz, t, N_HEADS, HEAD_DIM)
    v = proj[..., 2 * ATTN_WIDTH:3 * ATTN_WIDTH].reshape(bsz, t, N_HEADS, HEAD_DIM)
    f_logit = proj[..., 3 * ATTN_WIDTH:3 * ATTN_WIDTH + N_HEADS]
    u = proj[..., 3 * ATTN_WIDTH + N_HEADS:]
    q = rms_norm(q, g_q)
    k = rms_norm(k, g_k)
    logf = jax.nn.log_sigmoid((f_logit + b_f).astype(jnp.float32))
    if k_past is None:
        c = jnp.cumsum(logf, axis=1)
        attn = fox_prompt(q, k, v, c)
    else:
        past = k_past.shape[1]
        k_all = jnp.concatenate([k_past.astype(k.dtype), k], axis=1)
        v_all = jnp.concatenate([v_past.astype(v.dtype), v], axis=1)
        c_all = jnp.cumsum(jnp.concatenate([logf_past.astype(jnp.float32), logf], axis=1), axis=1)
        pos_k = jnp.arange(past + t)
        pos_q = past + jnp.arange(t)
        attn = fox_attend(q, k_all, v_all, c_all[:, past:], c_all, pos_q, pos_k).reshape(bsz, t, ATTN_WIDTH)
    ssm, s_re_new, s_im_new = s5_mixer(u, s_re, s_im, a_re, a_im, log_dt, b_re, b_im,
                                       c_re, c_im, d_skip, w_glu, b_glu)
    merged = jnp.concatenate([rms_norm(attn, g_attn_out), rms_norm(ssm, g_ssm_out)], axis=-1)
    h = x + merged @ w_out
    f = rms_norm(h, g_ffn)
    h = h + (jax.nn.silu(f @ w_gate) * (f @ w_up)) @ w_down
    gate = jax.nn.sigmoid(rms_norm(h, g_ple) @ w_ple_gate)
    h = h + gate * (p_i @ w_ple_proj)
    return h, k, v, logf, s_re_new, s_im_new


def setup_inputs(seed: int = 0) -> dict:
    key = jax.random.key(seed)
    ks = iter(jax.random.split(key, 48))
    f32 = jnp.float32

    def nrm(shape, scale):
        return scale * jax.random.normal(next(ks), shape, f32)

    def gain(shape):
        return 1.0 + nrm(shape, 0.02)

    L, G, P = DEPTH, N_SSM_GROUPS, SSM_STATE
    return {
        'x_prompt': nrm((BATCH, SEQ, D_MODEL), 1.0),
        'x_sample': nrm((DEC_BATCH, DEC_SEQ, D_MODEL), 1.0),
        'cache_k': nrm((L, DEC_BATCH, PAST_LEN, N_HEADS, HEAD_DIM), 1.0),
        'cache_v': nrm((L, DEC_BATCH, PAST_LEN, N_HEADS, HEAD_DIM), 1.0),
        'cache_logf': jax.nn.log_sigmoid(3.0 + nrm((L, DEC_BATCH, PAST_LEN, N_HEADS), 1.0)),
        'state_ssm_re': nrm((L, DEC_BATCH, G, P), 0.1),
        'state_ssm_im': nrm((L, DEC_BATCH, G, P), 0.1),
        'p_prompt': nrm((L, BATCH, SEQ, PLE_DIM), 1.0),
        'p_sample': nrm((L, DEC_BATCH, DEC_SEQ, PLE_DIM), 1.0),
        'g_mix': gain((L, D_MODEL)),
        'w_in': nrm((L, D_MODEL, IN_COLS), D_MODEL ** -0.5),
        'b_f': 3.0 + nrm((L, N_HEADS), 0.5),
        'g_q': gain((L, HEAD_DIM)),
        'g_k': gain((L, HEAD_DIM)),
        'a_re': -0.5 + nrm((L, G, P), 0.01),
        'a_im': math.pi * jnp.arange(P, dtype=f32)[None, None, :] + nrm((L, G, P), 0.01),
        'log_dt': jax.random.uniform(next(ks), (L, G), f32, math.log(1e-3), math.log(1e-1)),
        'b_re': nrm((L, G, P, SSM_GROUP), (2 * SSM_GROUP) ** -0.5),
        'b_im': nrm((L, G, P, SSM_GROUP), (2 * SSM_GROUP) ** -0.5),
        'c_re': nrm((L, G, SSM_GROUP, P), (2 * P) ** -0.5),
        'c_im': nrm((L, G, SSM_GROUP, P), (2 * P) ** -0.5),
        'd_skip': nrm((L, SSM_WIDTH), 1.0),
        'w_glu': nrm((L, SSM_WIDTH, SSM_WIDTH), SSM_WIDTH ** -0.5),
        'b_glu': nrm((L, SSM_WIDTH), 0.02),
        'g_attn_out': gain((L, ATTN_WIDTH)),
        'g_ssm_out': gain((L, SSM_WIDTH)),
        'w_out': nrm((L, MIX_WIDTH, D_MODEL), MIX_WIDTH ** -0.5),
        'g_ffn': gain((L, D_MODEL)),
        'w_gate': nrm((L, D_MODEL, D_FF), D_MODEL ** -0.5),
        'w_up': nrm((L, D_MODEL, D_FF), D_MODEL ** -0.5),
        'w_down': nrm((L, D_FF, D_MODEL), D_FF ** -0.5),
        'g_ple': gain((L, D_MODEL)),
        'w_ple_gate': nrm((L, D_MODEL, D_MODEL), D_MODEL ** -0.5),
        'w_ple_proj': nrm((L, PLE_DIM, D_MODEL), PLE_DIM ** -0.5),
    }


def reference(x_prompt, x_sample, cache_k, cache_v, cache_logf, state_ssm_re, state_ssm_im,
              p_prompt, p_sample, g_mix, w_in, b_f, g_q, g_k, a_re, a_im, log_dt, b_re, b_im,
              c_re, c_im, d_skip, w_glu, b_glu, g_attn_out, g_ssm_out, w_out, g_ffn, w_gate,
              w_up, w_down, g_ple, w_ple_gate, w_ple_proj):
    assert x_sample.shape[1] <= CHUNK
    zero_state = jnp.zeros((x_prompt.shape[0], N_SSM_GROUPS, SSM_STATE), jnp.float32)
    hp, hs = x_prompt, x_sample
    kp, vp, fp, srp, sip = [], [], [], [], []
    ksm, vsm, fsm, srs, sis = [], [], [], [], []
    for i in range(DEPTH):
        w = (g_mix[i], w_in[i], b_f[i], g_q[i], g_k[i], a_re[i], a_im[i], log_dt[i], b_re[i],
             b_im[i], c_re[i], c_im[i], d_skip[i], w_glu[i], b_glu[i], g_attn_out[i],
             g_ssm_out[i], w_out[i], g_ffn[i], w_gate[i], w_up[i], w_down[i], g_ple[i],
             w_ple_gate[i], w_ple_proj[i])
        hp, k1, v1, f1, r1, m1 = trunk_layer(hp, p_prompt[i], None, None, None,
                                             zero_state, zero_state, *w)
        hs, k2, v2, f2, r2, m2 = trunk_layer(hs, p_sample[i], cache_k[i], cache_v[i], cache_logf[i],
                                             state_ssm_re[i], state_ssm_im[i], *w)
        kp.append(k1); vp.append(v1); fp.append(f1); srp.append(r1); sip.append(m1)
        ksm.append(k2); vsm.append(v2); fsm.append(f2); srs.append(r2); sis.append(m2)
    y_prompt, y_sample = hp, hs
    k_prompt, v_prompt, logf_prompt = jnp.stack(kp), jnp.stack(vp), jnp.stack(fp)
    ssm_re_prompt, ssm_im_prompt = jnp.stack(srp), jnp.stack(sip)
    k_sample, v_sample, logf_sample = jnp.stack(ksm), jnp.stack(vsm), jnp.stack(fsm)
    ssm_re_sample, ssm_im_sample = jnp.stack(srs), jnp.stack(sis)
    return (y_prompt, y_sample, k_prompt, v_prompt, logf_prompt, ssm_re_prompt, ssm_im_prompt,
            k_sample, v_sample, logf_sample, ssm_re_sample, ssm_im_sample)
```

```python
import functools
import math

import jax
import jax.numpy as jnp
from jax import lax
from jax.experimental import pallas as pl
from jax.experimental.pallas import tpu as pltpu

F32 = jnp.float32
BF16 = jnp.bfloat16

EPS = 1e-6
NEG_INF = -1e30
LANES = 128
HEAD_DIM = 128
SSM_GROUP = 16
SSM_STATE = 64
S5_CHUNK = 16
GROUPS_PER_BLOCK = LANES // SSM_GROUP
STATE_HALF = GROUPS_PER_BLOCK * SSM_STATE
STATE_W = 2 * STATE_HALF
STATE_TILES = STATE_W // LANES
CHUNK_W = S5_CHUNK * LANES
VMEM_LIMIT = 56 * 1024 * 1024


def _params(sem):
    return pltpu.CompilerParams(dimension_semantics=sem, vmem_limit_bytes=VMEM_LIMIT)


def _rms(x, g):
    return x * lax.rsqrt(jnp.mean(x * x, axis=-1, keepdims=True) + EPS) * g


def _log_sigmoid(x):
    return jnp.minimum(x, 0.0) - jnp.log1p(jnp.exp(-jnp.abs(x)))


def _sigmoid(x):
    return 1.0 / (1.0 + jnp.exp(-x))


def _gelu_tanh(x):
    c = math.sqrt(2.0 / math.pi)
    return x * (0.5 * (1.0 + jnp.tanh(c * (x + 0.044715 * (x * x * x)))))


def _dot(a, b):
    return jnp.dot(a, b, preferred_element_type=F32)


def _dot_nt(a, b):
    return lax.dot_general(a, b, (((1,), (1,)), ((), ())), preferred_element_type=F32)


def _in_proj_kernel(x_ref, g_ref, wc_ref, wf_ref, bf_ref, gq_ref, gk_ref,
                    q_ref, k32_ref, kb_ref, v32_ref, vb_ref, u_ref, lf_ref, xn_ref,
                    *, tw, q_scale):
    j = pl.program_id(1)

    @pl.when(j == 0)
    def _():
        xn_ref[...] = _rms(x_ref[...], g_ref[...]).astype(BF16)
        lf_ref[...] = _log_sigmoid(_dot(xn_ref[...], wf_ref[...]) + bf_ref[...])

    acc = _dot(xn_ref[...], wc_ref[0])
    gq = gq_ref[...] * q_scale
    gk = gk_ref[...]
    for h in range(tw // HEAD_DIM):
        sl = slice(h * HEAD_DIM, (h + 1) * HEAD_DIM)
        qh = acc[:, h * HEAD_DIM:(h + 1) * HEAD_DIM]
        q_ref[:, sl] = (qh * lax.rsqrt(jnp.mean(qh * qh, axis=-1, keepdims=True) + EPS) * gq).astype(BF16)
        kh = acc[:, tw + h * HEAD_DIM:tw + (h + 1) * HEAD_DIM]
        kn = kh * lax.rsqrt(jnp.mean(kh * kh, axis=-1, keepdims=True) + EPS) * gk
        k32_ref[:, sl] = kn
        kb_ref[:, sl] = kn.astype(BF16)
    v = acc[:, 2 * tw:3 * tw]
    v32_ref[...] = v
    vb_ref[...] = v.astype(BF16)
    u_ref[...] = acc[:, 3 * tw:4 * tw]


def _in_proj(x, g_mix, wc, wf, bf, gq, gk, *, tm):
    n, d = x.shape
    nj, _, tw4 = wc.shape
    tw = tw4 // 4
    width = nj * tw
    grid = (n // tm, nj)
    tile = pl.BlockSpec((tm, tw), lambda i, j: (i, j))
    const = lambda shape: pl.BlockSpec(shape, lambda i, j: (0,) * len(shape))
    out_shape = (
        jax.ShapeDtypeStruct((n, width), BF16),
        jax.ShapeDtypeStruct((n, width), F32),
        jax.ShapeDtypeStruct((n, width), BF16),
        jax.ShapeDtypeStruct((n, width), F32),
        jax.ShapeDtypeStruct((n, width), BF16),
        jax.ShapeDtypeStruct((n, width), F32),
        jax.ShapeDtypeStruct((n, LANES), F32),
    )
    return pl.pallas_call(
        functools.partial(_in_proj_kernel, tw=tw, q_scale=HEAD_DIM ** -0.5),
        grid=grid,
        in_specs=[
            pl.BlockSpec((tm, d), lambda i, j: (i, 0)),
            const((1, d)),
            pl.BlockSpec((1, d, tw4), lambda i, j: (j, 0, 0)),
            const((d, LANES)),
            const((1, LANES)),
            const((1, HEAD_DIM)),
            const((1, HEAD_DIM)),
        ],
        out_specs=(tile, tile, tile, tile, tile, tile,
                   pl.BlockSpec((tm, LANES), lambda i, j: (i, 0))),
        out_shape=out_shape,
        scratch_shapes=[pltpu.VMEM((tm, d), BF16)],
        compiler_params=_params(("parallel", "arbitrary")),
        name="in_proj",
    )(x, g_mix, wc, wf, bf, gq, gk)


def _cumsum_kernel(lf_ref, init_ref, c_ref, carry_ref, *, tb):
    i = pl.program_id(1)

    @pl.when(i == 0)
    def _():
        carry_ref[...] = init_ref[0]

    x = lf_ref[0]
    row = lax.broadcasted_iota(jnp.int32, (tb, tb), 0)
    col = lax.broadcasted_iota(jnp.int32, (tb, tb), 1)
    tri = jnp.where(row <= col, 1.0, 0.0).astype(BF16)
    x1 = x.astype(BF16)
    r1 = x - x1.astype(F32)
    x2 = r1.astype(BF16)
    x3 = (r1 - x2.astype(F32)).astype(BF16)
    cs = _dot(x1, tri) + _dot(x2, tri) + _dot(x3, tri) + carry_ref[:, 0:1]
    c_ref[0] = cs
    carry_ref[...] = jnp.broadcast_to(cs[:, tb - 1:tb], carry_ref.shape)


def _cumsum(lf, init, *, tb):
    b, h, t = lf.shape
    return pl.pallas_call(
        functools.partial(_cumsum_kernel, tb=tb),
        grid=(b, t // tb),
        in_specs=[pl.BlockSpec((1, h, tb), lambda i, j: (i, 0, j)),
                  pl.BlockSpec((1, h, LANES), lambda i, j: (i, 0, 0))],
        out_specs=pl.BlockSpec((1, h, tb), lambda i, j: (i, 0, j)),
        out_shape=jax.ShapeDtypeStruct((b, h, t), F32),
        scratch_shapes=[pltpu.VMEM((h, LANES), F32)],
        compiler_params=_params(("parallel", "arbitrary")),
        name="cumsum",
    )(lf, init)


def _online_softmax_step(s, v, m_ref, l_ref, acc_ref, sl):
    m_prev = m_ref[:, sl]
    m_new = jnp.maximum(m_prev, jnp.max(s, axis=-1, keepdims=True))
    alpha = jnp.exp(m_prev - m_new)
    p = jnp.exp(s - m_new[:, 0:1])
    l_ref[:, sl] = alpha * l_ref[:, sl] + jnp.sum(p, axis=-1, keepdims=True)
    acc_ref[:, sl] = alpha * acc_ref[:, sl] + _dot(p.astype(BF16), v)
    m_ref[:, sl] = m_new


def _attn_kernel(qi_tab, kj_tab, q_ref, k_ref, v_ref, ck_ref, o_ref, m_ref, l_ref, acc_ref, *, tq):
    p = pl.program_id(2)
    qi = qi_tab[p]
    kj = kj_tab[p]

    @pl.when(kj == 0)
    def _():
        m_ref[...] = jnp.full(m_ref.shape, NEG_INF, F32)
        l_ref[...] = jnp.zeros(l_ref.shape, F32)
        acc_ref[...] = jnp.zeros(acc_ref.shape, F32)

    s = _dot_nt(q_ref[0], k_ref[0]) - ck_ref[0]
    row = lax.broadcasted_iota(jnp.int32, s.shape, 0)
    col = lax.broadcasted_iota(jnp.int32, s.shape, 1)
    s = jnp.where(col <= row + (qi - kj) * tq, s, NEG_INF)
    _online_softmax_step(s, v_ref[0], m_ref, l_ref, acc_ref, slice(None))

    @pl.when(kj == qi)
    def _():
        o_ref[0] = acc_ref[...] / l_ref[...]


def _attention(q, k, v, ck, *, tq):
    b, t, width = q.shape
    nh = width // HEAD_DIM
    nq = t // tq
    pairs = [(i, j) for i in range(nq) for j in range(i + 1)]
    qi_tab = jnp.asarray([p[0] for p in pairs], jnp.int32)
    kj_tab = jnp.asarray([p[1] for p in pairs], jnp.int32)
    grid_spec = pltpu.PrefetchScalarGridSpec(
        num_scalar_prefetch=2,
        grid=(b, nh, len(pairs)),
        in_specs=[
            pl.BlockSpec((1, tq, HEAD_DIM), lambda bi, h, p, qt, kt: (bi, qt[p], h)),
            pl.BlockSpec((1, tq, HEAD_DIM), lambda bi, h, p, qt, kt: (bi, kt[p], h)),
            pl.BlockSpec((1, tq, HEAD_DIM), lambda bi, h, p, qt, kt: (bi, kt[p], h)),
            pl.BlockSpec((1, 1, tq), lambda bi, h, p, qt, kt: (bi * nh + h, 0, kt[p])),
        ],
        out_specs=pl.BlockSpec((1, tq, HEAD_DIM), lambda bi, h, p, qt, kt: (bi, qt[p], h)),
        scratch_shapes=[pltpu.VMEM((tq, HEAD_DIM), F32)] * 3,
    )
    return pl.pallas_call(
        functools.partial(_attn_kernel, tq=tq),
        grid_spec=grid_spec,
        out_shape=jax.ShapeDtypeStruct((b, t, width), F32),
        compiler_params=_params(("parallel", "parallel", "arbitrary")),
        name="attention",
    )(qi_tab, kj_tab, q, k, v, ck)


def _attn_cached_kernel(q_ref, kc_ref, vc_ref, ckp_ref, kn_ref, vn_ref, ckn_ref,
                        o_ref, m_ref, l_ref, acc_ref, *, nh, nkv):
    j = pl.program_id(1)

    @pl.when(j == 0)
    def _():
        m_ref[...] = jnp.full(m_ref.shape, NEG_INF, F32)
        l_ref[...] = jnp.zeros(l_ref.shape, F32)
        acc_ref[...] = jnp.zeros(acc_ref.shape, F32)

    for h in range(nh):
        sl = slice(h * HEAD_DIM, (h + 1) * HEAD_DIM)
        kh = kc_ref[0, :, sl].astype(BF16)
        vh = vc_ref[0, :, sl].astype(BF16)
        s = _dot_nt(q_ref[0, :, sl], kh) - ckp_ref[0, h:h + 1, :]
        _online_softmax_step(s, vh, m_ref, l_ref, acc_ref, sl)

    @pl.when(j == nkv - 1)
    def _():
        for h in range(nh):
            sl = slice(h * HEAD_DIM, (h + 1) * HEAD_DIM)
            s = _dot_nt(q_ref[0, :, sl], kn_ref[0, :, sl]) - ckn_ref[0, h:h + 1, :]
            row = lax.broadcasted_iota(jnp.int32, s.shape, 0)
            col = lax.broadcasted_iota(jnp.int32, s.shape, 1)
            s = jnp.where(col <= row, s, NEG_INF)
            _online_softmax_step(s, vn_ref[0, :, sl], m_ref, l_ref, acc_ref, sl)
        o_ref[0] = acc_ref[...] / l_ref[...]


def _attention_cached(q, k_cache, v_cache, ck_past, k_new, v_new, ck_new, *, tk):
    b, t, width = q.shape
    nh = width // HEAD_DIM
    past = k_cache.shape[1]
    nkv = past // tk
    per_b = lambda shape: pl.BlockSpec(shape, lambda i, j: (i, 0, 0))
    return pl.pallas_call(
        functools.partial(_attn_cached_kernel, nh=nh, nkv=nkv),
        grid=(b, nkv),
        in_specs=[
            per_b((1, t, width)),
            pl.BlockSpec((1, tk, width), lambda i, j: (i, j, 0)),
            pl.BlockSpec((1, tk, width), lambda i, j: (i, j, 0)),
            pl.BlockSpec((1, nh, tk), lambda i, j: (i, 0, j)),
            per_b((1, t, width)),
            per_b((1, t, width)),
            per_b((1, nh, t)),
        ],
        out_specs=per_b((1, t, width)),
        out_shape=jax.ShapeDtypeStruct((b, t, width), F32),
        scratch_shapes=[pltpu.VMEM((t, width), F32)] * 3,
        compiler_params=_params(("parallel", "arbitrary")),
        name="attention_cached",
    )(q, k_cache, v_cache, ck_past, k_new, v_new, ck_new)


def _s5_prep_kernel(are_ref, aim_ref, ldt_ref, bre_ref, bim_ref, cre_ref, cim_ref,
                    kall_ref, bpre_ref, bpim_ref, cpre_ref, cpim_ref, a16_ref):
    tau = lax.broadcasted_iota(jnp.int32, (S5_CHUNK + 1, 1, 1), 0).astype(F32)
    tau_rev = (S5_CHUNK - 1) - lax.broadcasted_iota(jnp.int32, (S5_CHUNK, 1, 1), 0).astype(F32)
    for g in range(GROUPS_PER_BLOCK):
        ar = are_ref[0, g:g + 1, :]
        ai = aim_ref[0, g:g + 1, :]
        dt = jnp.exp(ldt_ref[0, g:g + 1, :])
        ard = (ar * dt)[None]
        aid = (ai * dt)[None]
        mag = jnp.exp(tau * ard)
        pw_re = mag * jnp.cos(tau * aid)
        pw_im = mag * jnp.sin(tau * aid)
        nr = pw_re[1] - 1.0
        ni = pw_im[1]
        den = ar * ar + ai * ai
        coef_re = (nr * ar + ni * ai) / den
        coef_im = (ni * ar - nr * ai) / den
        b_re = bre_ref[0, g]
        b_im = bim_ref[0, g]
        bb_re = coef_re * b_re - coef_im * b_im
        bb_im = coef_re * b_im + coef_im * b_re
        c_re = cre_ref[0, g][None]
        c_im = cim_ref[0, g][None]
        cp_re = c_re * pw_re - c_im * pw_im
        cp_im = c_re * pw_im + c_im * pw_re
        cpre_ref[0, g] = cp_re[1:]
        cpim_ref[0, g] = cp_im[1:]
        lhs_re = cp_re[:S5_CHUNK].reshape(S5_CHUNK * SSM_GROUP, SSM_STATE)
        lhs_im = cp_im[:S5_CHUNK].reshape(S5_CHUNK * SSM_GROUP, SSM_STATE)
        hp = lax.Precision.HIGHEST
        kall_ref[0, g] = (
            lax.dot_general(lhs_re, bb_re, (((1,), (1,)), ((), ())), precision=hp,
                            preferred_element_type=F32)
            - lax.dot_general(lhs_im, bb_im, (((1,), (1,)), ((), ())), precision=hp,
                              preferred_element_type=F32))
        mag_r = jnp.exp(tau_rev * ard)
        pr_re = mag_r * jnp.cos(tau_rev * aid)
        pr_im = mag_r * jnp.sin(tau_rev * aid)
        bpre_ref[0, g] = pr_re * bb_re[None] - pr_im * bb_im[None]
        bpim_ref[0, g] = pr_re * bb_im[None] + pr_im * bb_re[None]
        a16_ref[0, g] = jnp.concatenate([pw_re[S5_CHUNK], pw_im[S5_CHUNK]], axis=0)


def _s5_prep(a_re, a_im, log_dt, b_re, b_im, c_re, c_im):
    nl, ng, ns = a_re.shape
    gpb = GROUPS_PER_BLOCK
    n = nl * ng // gpb
    blk = lambda a: a.reshape((n, gpb) + a.shape[2:])
    args = (blk(a_re), blk(a_im), blk(log_dt[..., None]),
            blk(jnp.swapaxes(b_re, -1, -2)), blk(jnp.swapaxes(b_im, -1, -2)),
            blk(c_re), blk(c_im))
    spec = lambda shape: pl.BlockSpec((1,) + shape, lambda i: (i,) + (0,) * len(shape))
    c, p, L = SSM_GROUP, SSM_STATE, S5_CHUNK
    out_dims = [(gpb, L * c, c), (gpb, L, c, p), (gpb, L, c, p), (gpb, L, c, p), (gpb, L, c, p),
                (gpb, 2, p)]
    kall, bp_re, bp_im, cp_re, cp_im, a16 = pl.pallas_call(
        _s5_prep_kernel,
        grid=(n,),
        in_specs=[spec((gpb, p)), spec((gpb, p)), spec((gpb, 1)),
                  spec((gpb, c, p)), spec((gpb, c, p)), spec((gpb, c, p)), spec((gpb, c, p))],
        out_specs=tuple(spec(d) for d in out_dims),
        out_shape=tuple(jax.ShapeDtypeStruct((n,) + d, F32) for d in out_dims),
        compiler_params=_params(("parallel",)),
        name="s5_prep",
    )(*args)
    eye = jnp.eye(gpb, dtype=F32)
    kall = kall.reshape(n, gpb, L, c, c)
    r_mat = jnp.einsum("ngtdc,gh->ntgchd", kall, eye).reshape(n, L, LANES, LANES)
    r_mat = r_mat[:, ::-1].reshape(n, L * LANES, LANES).astype(BF16)
    st = lambda a: jnp.einsum("ngscp,gh->nsgchp", a, eye).reshape(n, L * LANES, STATE_HALF)
    w_st = jnp.concatenate([st(bp_re), st(bp_im)], axis=-1).astype(BF16)
    ot = lambda a: jnp.einsum("ngtcp,gh->ngpthc", a, eye).reshape(n, STATE_HALF, L * LANES)
    w_out = jnp.concatenate([ot(cp_re), -ot(cp_im)], axis=1).astype(BF16)
    a16 = jnp.swapaxes(a16, 1, 2).reshape(n * STATE_TILES, 1, LANES)
    return r_mat, w_st, w_out, a16


def _gather_chunk_inputs(u_ref, uflat_ref, tr):
    for s in range(S5_CHUNK):
        uflat_ref[:, s * LANES:(s + 1) * LANES] = (
            u_ref[pl.ds(s, tr, stride=S5_CHUNK), :].astype(BF16))


def _s5_state_kernel(u_ref, wst_ref, s_ref, uflat_ref, *, tr):
    _gather_chunk_inputs(u_ref, uflat_ref, tr)
    s = _dot(uflat_ref[...], wst_ref[0])
    for j in range(STATE_TILES):
        s_ref[j] = s[:, j * LANES:(j + 1) * LANES]


def _s5_state(u, w_st, *, tr):
    n, width = u.shape
    rows = n // S5_CHUNK
    nblk = width // LANES
    return pl.pallas_call(
        functools.partial(_s5_state_kernel, tr=tr),
        grid=(nblk, rows // tr),
        in_specs=[pl.BlockSpec((tr * S5_CHUNK, LANES), lambda g, i: (i, g)),
                  pl.BlockSpec((1, CHUNK_W, STATE_W), lambda g, i: (g, 0, 0))],
        out_specs=pl.BlockSpec((STATE_TILES, tr, LANES), lambda g, i: (g, i, 0)),
        out_shape=jax.ShapeDtypeStruct((nblk * STATE_TILES, rows, LANES), F32),
        scratch_shapes=[pltpu.VMEM((tr, CHUNK_W), BF16)],
        compiler_params=_params(("parallel", "parallel")),
        name="s5_state",
    )(u, w_st)


def _s5_scan_kernel(s_ref, a_ref, x0_ref, xp_ref, xf_ref, *, nb, nck):
    half = STATE_TILES // 2
    a = [a_ref[j] for j in range(STATE_TILES)]

    def body(k, x):
        rows = pl.ds(k, nb, stride=nck)
        new = [None] * STATE_TILES
        for j in range(half):
            xp_ref.at[j][rows, :] = x[j]
            xp_ref.at[half + j][rows, :] = x[half + j]
            new[j] = a[j] * x[j] - a[half + j] * x[half + j] + s_ref.at[j][rows, :]
            new[half + j] = a[j] * x[half + j] + a[half + j] * x[j] + s_ref.at[half + j][rows, :]
        return tuple(new)

    x = lax.fori_loop(0, nck, body, tuple(x0_ref[j] for j in range(STATE_TILES)))
    for j in range(STATE_TILES):
        xf_ref[j] = x[j]


def _s5_scan(s, a16, x0, *, nb):
    tiles, rows, _ = s.shape
    nblk = tiles // STATE_TILES
    nck = rows // nb
    blk = lambda r: pl.BlockSpec((STATE_TILES, r, LANES), lambda g: (g, 0, 0))
    return pl.pallas_call(
        functools.partial(_s5_scan_kernel, nb=nb, nck=nck),
        grid=(nblk,),
        in_specs=[blk(rows), blk(1), blk(nb)],
        out_specs=(blk(rows), blk(nb)),
        out_shape=(jax.ShapeDtypeStruct((tiles, rows, LANES), F32),
                   jax.ShapeDtypeStruct((tiles, nb, LANES), F32)),
        compiler_params=_params(("parallel",)),
        name="s5_scan",
    )(s, a16, x0)


def _s5_out_kernel(u_ref, xp_ref, r_ref, wout_ref, d_ref, g_ref, uflat_ref, *, tr):
    _gather_chunk_inputs(u_ref, uflat_ref, tr)
    x_prev = jnp.concatenate([xp_ref[j].astype(BF16) for j in range(STATE_TILES)], axis=1)
    carried = _dot(x_prev, wout_ref[0])
    d = d_ref[...]
    for t in range(S5_CHUNK):
        rows = pl.ds(t, tr, stride=S5_CHUNK)
        y = (carried[:, t * LANES:(t + 1) * LANES]
             + _dot(uflat_ref[:, :(t + 1) * LANES], r_ref[0, (S5_CHUNK - 1 - t) * LANES:, :])
             + d * u_ref[rows, :])
        g_ref[rows, :] = _gelu_tanh(y)


def _s5_out(u, x_prev, r_mat, w_out, d_skip, *, tr):
    n, width = u.shape
    rows = n // S5_CHUNK
    nblk = width // LANES
    return pl.pallas_call(
        functools.partial(_s5_out_kernel, tr=tr),
        grid=(nblk, rows // tr),
        in_specs=[pl.BlockSpec((tr * S5_CHUNK, LANES), lambda g, i: (i, g)),
                  pl.BlockSpec((STATE_TILES, tr, LANES), lambda g, i: (g, i, 0)),
                  pl.BlockSpec((1, CHUNK_W, LANES), lambda g, i: (g, 0, 0)),
                  pl.BlockSpec((1, STATE_W, CHUNK_W), lambda g, i: (g, 0, 0)),
                  pl.BlockSpec((1, LANES), lambda g, i: (0, g))],
        out_specs=pl.BlockSpec((tr * S5_CHUNK, LANES), lambda g, i: (i, g)),
        out_shape=jax.ShapeDtypeStruct((n, width), F32),
        scratch_shapes=[pltpu.VMEM((tr, CHUNK_W), BF16)],
        compiler_params=_params(("parallel", "parallel")),
        name="s5_out",
    )(u, x_prev, r_mat, w_out, d_skip)


def _mix_out_kernel(attn_ref, g_ref, x_ref, wglu_ref, bglu_ref, gat_ref, gss_ref,
                    wo_a_ref, wo_s_ref, h_ref):
    g = g_ref[...]
    gate = _sigmoid(_dot(g.astype(BF16), wglu_ref[...]) + bglu_ref[...])
    ssm_n = _rms(g * gate, gss_ref[...]).astype(BF16)
    attn_n = _rms(attn_ref[...], gat_ref[...]).astype(BF16)
    h_ref[...] = x_ref[...] + _dot(attn_n, wo_a_ref[...]) + _dot(ssm_n, wo_s_ref[...])


def _mix_out(attn, g, x, w_glu, b_glu, g_attn, g_ssm, wo_a, wo_s, *, tm):
    n, d = x.shape
    wa = attn.shape[1]
    ws = g.shape[1]
    const = lambda shape: pl.BlockSpec(shape, lambda i: (0, 0))
    return pl.pallas_call(
        _mix_out_kernel,
        grid=(n // tm,),
        in_specs=[pl.BlockSpec((tm, wa), lambda i: (i, 0)),
                  pl.BlockSpec((tm, ws), lambda i: (i, 0)),
                  pl.BlockSpec((tm, d), lambda i: (i, 0)),
                  const((ws, ws)), const((1, ws)), const((1, wa)), const((1, ws)),
                  const((wa, d)), const((ws, d))],
        out_specs=pl.BlockSpec((tm, d), lambda i: (i, 0)),
        out_shape=jax.ShapeDtypeStruct((n, d), F32),
        compiler_params=_params(("parallel",)),
        name="mix_out",
    )(attn, g, x, w_glu, b_glu, g_attn, g_ssm, wo_a, wo_s)


def _ffn_kernel(h_ref, g_ref, wg_ref, wu_ref, wd_ref, o_ref, fn_ref, acc_ref, *, nf):
    f = pl.program_id(1)

    @pl.when(f == 0)
    def _():
        fn_ref[...] = _rms(h_ref[...], g_ref[...]).astype(BF16)
        acc_ref[...] = h_ref[...]

    fn = fn_ref[...]
    gate = _dot(fn, wg_ref[...])
    up = _dot(fn, wu_ref[...])
    act = (gate * _sigmoid(gate) * up).astype(BF16)
    acc_ref[...] += _dot(act, wd_ref[...])

    @pl.when(f == nf - 1)
    def _():
        o_ref[...] = acc_ref[...]


def _ffn(h, g_ffn, w_gate, w_up, w_down, *, tm, tf):
    n, d = h.shape
    dff = w_gate.shape[1]
    nf = dff // tf
    return pl.pallas_call(
        functools.partial(_ffn_kernel, nf=nf),
        grid=(n // tm, nf),
        in_specs=[pl.BlockSpec((tm, d), lambda i, f: (i, 0)),
                  pl.BlockSpec((1, d), lambda i, f: (0, 0)),
                  pl.BlockSpec((d, tf), lambda i, f: (0, f)),
                  pl.BlockSpec((d, tf), lambda i, f: (0, f)),
                  pl.BlockSpec((tf, d), lambda i, f: (f, 0))],
        out_specs=pl.BlockSpec((tm, d), lambda i, f: (i, 0)),
        out_shape=jax.ShapeDtypeStruct((n, d), F32),
        scratch_shapes=[pltpu.VMEM((tm, d), BF16), pltpu.VMEM((tm, d), F32)],
        compiler_params=_params(("parallel", "arbitrary")),
        name="ffn",
    )(h, g_ffn, w_gate, w_up, w_down)


def _ple_kernel(h_ref, p_ref, g_ref, wg_ref, wp_ref, o_ref):
    h = h_ref[...]
    gate = _sigmoid(_dot(_rms(h, g_ref[...]).astype(BF16), wg_ref[...]))
    o_ref[...] = h + gate * _dot(p_ref[...].astype(BF16), wp_ref[...])


def _ple(h, p, g_ple, w_gate, w_proj, *, tm):
    n, d = h.shape
    dp = p.shape[1]
    const = lambda shape: pl.BlockSpec(shape, lambda i: (0, 0))
    return pl.pallas_call(
        _ple_kernel,
        grid=(n // tm,),
        in_specs=[pl.BlockSpec((tm, d), lambda i: (i, 0)),
                  pl.BlockSpec((tm, dp), lambda i: (i, 0)),
                  const((1, d)), const((d, d)), const((dp, d))],
        out_specs=pl.BlockSpec((tm, d), lambda i: (i, 0)),
        out_shape=jax.ShapeDtypeStruct((n, d), F32),
        compiler_params=_params(("parallel",)),
        name="ple",
    )(h, p, g_ple, w_gate, w_proj)


def _tile(n, pref):
    t = min(n, pref)
    while n % t:
        t -= 8
    return t


def _layer_weights(i, nh, g_mix, w_in, b_f, g_q, g_k, d_skip, w_glu, b_glu, g_attn_out,
                   g_ssm_out, w_out, g_ffn, w_gate, w_up, w_down, g_ple, w_ple_gate, w_ple_proj):
    d = w_in.shape[1]
    wa = nh * HEAD_DIM
    ws = d_skip.shape[1]
    assert wa == ws, "q/k/v/u column tiles are interleaved assuming equal group widths"
    tw = min(2 * HEAD_DIM, wa)
    w = w_in[i]
    parts = [w[:, :wa], w[:, wa:2 * wa], w[:, 2 * wa:3 * wa], w[:, 3 * wa + nh:]]
    wc = jnp.stack([jnp.concatenate([p[:, j * tw:(j + 1) * tw] for p in parts], axis=1)
                    for j in range(wa // tw)]).astype(BF16)
    wf = jnp.pad(w[:, 3 * wa:3 * wa + nh], ((0, 0), (0, LANES - nh))).astype(BF16)
    bf = jnp.pad(b_f[i], (0, LANES - nh))[None]
    row = lambda a: a[i][None]
    return dict(
        g_mix=row(g_mix), wc=wc, wf=wf, bf=bf, g_q=row(g_q), g_k=row(g_k), d_skip=row(d_skip),
        w_glu=w_glu[i].astype(BF16), b_glu=row(b_glu), g_attn=row(g_attn_out), g_ssm=row(g_ssm_out),
        wo_a=w_out[i][:wa].astype(BF16), wo_s=w_out[i][wa:].astype(BF16), g_ffn=row(g_ffn),
        w_gate=w_gate[i].astype(BF16), w_up=w_up[i].astype(BF16), w_down=w_down[i].astype(BF16),
        g_ple=row(g_ple), w_ple_gate=w_ple_gate[i].astype(BF16), w_ple_proj=w_ple_proj[i].astype(BF16))


def _layer(x, p_i, cache, state0, w, s5, nh):
    b, t, d = x.shape
    n = b * t
    wa = nh * HEAD_DIM
    tm = _tile(n, 512)
    xf = x.reshape(n, d)
    q, k32, kb, v32, vb, u, lf = _in_proj(xf, w["g_mix"], w["wc"], w["wf"], w["bf"],
                                          w["g_q"], w["g_k"], tm=tm)
    logf = lf[:, :nh].reshape(b, t, nh)
    lf_rows = jnp.swapaxes(logf, 1, 2)
    zero_init = jnp.zeros((b, nh, LANES), F32)
    seq = lambda a: a.reshape(b, t, wa)
    if cache is None:
        ck = _cumsum(lf_rows, zero_init, tb=_tile(t, 256))
        attn = _attention(seq(q), seq(kb), seq(vb), ck.reshape(b * nh, 1, t), tq=_tile(t, 512))
    else:
        k_past, v_past, lf_past = cache
        past = k_past.shape[1]
        ck_past = _cumsum(jnp.swapaxes(lf_past, 1, 2), zero_init, tb=_tile(past, 256))
        init = jnp.broadcast_to(ck_past[:, :, past - 1:past], (b, nh, LANES))
        ck_new = _cumsum(jnp.pad(lf_rows, ((0, 0), (0, 0), (0, LANES - t))), init, tb=LANES)
        attn = _attention_cached(seq(q), k_past.reshape(b, past, wa), v_past.reshape(b, past, wa),
                                 ck_past, seq(kb), seq(vb), ck_new[:, :, :t], tk=_tile(past, 512))
    attn = attn.reshape(n, wa)

    r_mat, w_st, w_so, a16 = s5
    rows = n // S5_CHUNK
    tr = _tile(rows, 256)
    s_contrib = _s5_state(u, w_st, tr=tr)
    x_prev, x_fin = _s5_scan(s_contrib, a16, state0, nb=b)
    g = _s5_out(u, x_prev, r_mat, w_so, w["d_skip"], tr=tr)

    h = _mix_out(attn, g, xf, w["w_glu"], w["b_glu"], w["g_attn"], w["g_ssm"],
                 w["wo_a"], w["wo_s"], tm=_tile(n, 256))
    h = _ffn(h, w["g_ffn"], w["w_gate"], w["w_up"], w["w_down"], tm=tm,
             tf=_tile(w["w_gate"].shape[1], 512))
    h = _ple(h, p_i.reshape(n, -1), w["g_ple"], w["w_ple_gate"], w["w_ple_proj"], tm=tm)

    nblk = x_fin.shape[0] // STATE_TILES
    xs = x_fin.reshape(nblk, 2, STATE_TILES // 2, b, LANES).transpose(3, 0, 1, 2, 4)
    xs = xs.reshape(b, nblk, 2, GROUPS_PER_BLOCK, SSM_STATE)
    s_re = xs[:, :, 0].reshape(b, nblk * GROUPS_PER_BLOCK, SSM_STATE)
    s_im = xs[:, :, 1].reshape(b, nblk * GROUPS_PER_BLOCK, SSM_STATE)
    shape4 = (b, t, nh, HEAD_DIM)
    return h.reshape(b, t, d), k32.reshape(shape4), v32.reshape(shape4), logf, s_re, s_im


def _pack_state(s_re, s_im):
    b, g, p = s_re.shape
    nblk = g // GROUPS_PER_BLOCK
    half = STATE_TILES // 2
    both = jnp.stack([s_re.reshape(b, nblk, half, LANES), s_im.reshape(b, nblk, half, LANES)], axis=2)
    return both.transpose(1, 2, 3, 0, 4).reshape(nblk * STATE_TILES, b, LANES)


def kernel(x_prompt, x_sample, cache_k, cache_v, cache_logf, state_ssm_re, state_ssm_im, p_prompt, p_sample, g_mix, w_in, b_f, g_q, g_k, a_re, a_im, log_dt, b_re, b_im, c_re, c_im, d_skip, w_glu, b_glu, g_attn_out, g_ssm_out, w_out, g_ffn, w_gate, w_up, w_down, g_ple, w_ple_gate, w_ple_proj):
    depth = w_in.shape[0]
    nh = b_f.shape[1]
    ng = a_re.shape[1]
    nblk = ng // GROUPS_PER_BLOCK
    r_mat, w_st, w_so, a16 = _s5_prep(a_re, a_im, log_dt, b_re, b_im, c_re, c_im)
    zero_state = jnp.zeros((nblk * STATE_TILES, x_prompt.shape[0], LANES), F32)
    hp, hs = x_prompt, x_sample
    outs_p, outs_s = [], []
    for i in range(depth):
        w = _layer_weights(i, nh, g_mix, w_in, b_f, g_q, g_k, d_skip, w_glu, b_glu, g_attn_out,
                           g_ssm_out, w_out, g_ffn, w_gate, w_up, w_down, g_ple, w_ple_gate,
                           w_ple_proj)
        blk = slice(i * nblk, (i + 1) * nblk)
        tiles = slice(i * nblk * STATE_TILES, (i + 1) * nblk * STATE_TILES)
        s5 = (r_mat[blk], w_st[blk], w_so[blk], a16[tiles])
        hp, *rest_p = _layer(hp, p_prompt[i], None, zero_state, w, s5, nh)
        hs, *rest_s = _layer(hs, p_sample[i], (cache_k[i], cache_v[i], cache_logf[i]),
                             _pack_state(state_ssm_re[i], state_ssm_im[i]), w, s5, nh)
        outs_p.append(rest_p)
        outs_s.append(rest_s)
    stack = lambda outs, j: jnp.stack([o[j] for o in outs])
    return (hp, hs,
            stack(outs_p, 0), stack(outs_p, 1), stack(outs_p, 2), stack(outs_p, 3), stack(outs_p, 4),
            stack(outs_s, 0), stack(outs_s, 1), stack(outs_s, 2), stack(outs_s, 3), stack(outs_s, 4))
```

```python
import functools
import math

import jax
import jax.numpy as jnp
from jax import lax
from jax.experimental import pallas as pl
from jax.experimental.pallas import tpu as pltpu

F32 = jnp.float32
BF16 = jnp.bfloat16

EPS = 1e-6
NEG_INF = -1e30
LOG2E = math.log2(math.e)
LANES = 128
HEAD_DIM = 128
SSM_GROUP = 16
SSM_STATE = 64
S5_CHUNK = 16
GROUPS_PER_BLOCK = LANES // SSM_GROUP
STATE_HALF = GROUPS_PER_BLOCK * SSM_STATE
STATE_W = 2 * STATE_HALF
STATE_TILES = STATE_W // LANES
CHUNK_W = S5_CHUNK * LANES
VMEM_LIMIT = 56 * 1024 * 1024


def _params(sem):
    return pltpu.CompilerParams(dimension_semantics=sem, vmem_limit_bytes=VMEM_LIMIT)


def _tile(n, pref, step=8):
    if n <= pref:
        return n
    t = pref // step * step
    while n % t:
        t -= step
    return t


def _rms(x, g):
    return x * lax.rsqrt(jnp.mean(x * x, axis=-1, keepdims=True) + EPS) * g


def _log_sigmoid(x):
    return jnp.minimum(x, 0.0) - jnp.log1p(jnp.exp(-jnp.abs(x)))


def _sigmoid(x):
    return 1.0 / (1.0 + jnp.exp(-x))


def _gelu_tanh(x):
    c = math.sqrt(2.0 / math.pi)
    return x * (0.5 * (1.0 + jnp.tanh(c * (x + 0.044715 * (x * x * x)))))


def _dot(a, b):
    return jnp.dot(a, b, preferred_element_type=F32)


def _dot_nt(a, b):
    return lax.dot_general(a, b, (((1,), (1,)), ((), ())), preferred_element_type=F32)


def _layer_vec(width):
    return lambda l: pl.BlockSpec((None, 1, width), lambda *_: (l, 0, 0))


def _cast_kernel(x_ref, o_ref):
    o_ref[...] = x_ref[...].astype(BF16)


def _cast_bf16(w, cols=None):
    nl, r, c = w.shape
    cols = cols or c
    tr = _tile(r, 512)
    tc = _tile(cols, 2048, LANES)
    spec = pl.BlockSpec((None, tr, tc), lambda l, i, j: (l, i, j))
    return pl.pallas_call(
        _cast_kernel,
        grid=(nl, r // tr, cols // tc),
        in_specs=[spec],
        out_specs=spec,
        out_shape=jax.ShapeDtypeStruct((nl, r, cols), BF16),
        compiler_params=_params(("parallel", "parallel", "parallel")),
        name="cast_bf16",
    )(w)


def _in_proj_kernel(*refs, tw, q_scale, n_alias):
    (x_ref, g_ref, wq_ref, wk_ref, wv_ref, wu_ref, wf_ref, bf_ref, gq_ref, gk_ref) = refs[:10]
    (q_ref, k32_ref, kb_ref, v32_ref, vb_ref, u_ref, lf_ref, xn_ref) = refs[10 + n_alias:]
    j = pl.program_id(1)

    @pl.when(j == 0)
    def _():
        xn_ref[...] = _rms(x_ref[...], g_ref[...]).astype(BF16)
        lf_ref[...] = _log_sigmoid(_dot(xn_ref[...], wf_ref[...]) + bf_ref[...])

    xn = xn_ref[...]
    q = _dot(xn, wq_ref[...])
    k = _dot(xn, wk_ref[...])
    gq = gq_ref[...] * q_scale
    gk = gk_ref[...]
    for h in range(tw // HEAD_DIM):
        sl = slice(h * HEAD_DIM, (h + 1) * HEAD_DIM)
        qh = q[:, sl]
        q_ref[:, sl] = (qh * lax.rsqrt(jnp.mean(qh * qh, axis=-1, keepdims=True) + EPS) * gq).astype(BF16)
        kh = k[:, sl]
        kn = kh * lax.rsqrt(jnp.mean(kh * kh, axis=-1, keepdims=True) + EPS) * gk
        k32_ref[:, sl] = kn
        kb_ref[:, sl] = kn.astype(BF16)
    v = _dot(xn, wv_ref[...])
    v32_ref[...] = v
    vb_ref[...] = v.astype(BF16)
    u_ref[...] = _dot(xn, wu_ref[...])


def _in_proj(l, x, w, kv_all, *, tm):
    n, d = x.shape
    nl = w["w_qkv"].shape[0]
    width = w["w_u"].shape[2]
    tw = min(2 * HEAD_DIM, width)
    nj = width // tw
    tile = pl.BlockSpec((tm, tw), lambda i, j: (i, j))
    stacked = pl.BlockSpec((None, tm, tw), lambda i, j: (l, i, j))
    wcol = lambda off: pl.BlockSpec((None, d, tw), lambda i, j: (l, 0, off * nj + j))
    n_alias = 0 if kv_all is None else 2
    in_specs = [
        pl.BlockSpec((tm, d), lambda i, j: (i, 0)),
        _layer_vec(d)(l),
        wcol(0), wcol(1), wcol(2), wcol(0),
        pl.BlockSpec((None, d, LANES), lambda i, j: (l, 0, 0)),
        _layer_vec(LANES)(l), _layer_vec(HEAD_DIM)(l), _layer_vec(HEAD_DIM)(l),
    ] + [pl.BlockSpec(memory_space=pl.ANY)] * n_alias
    out_shape = (
        jax.ShapeDtypeStruct((n, width), BF16),
        jax.ShapeDtypeStruct((nl, n, width), F32),
        jax.ShapeDtypeStruct((n, width), BF16),
        jax.ShapeDtypeStruct((nl, n, width), F32),
        jax.ShapeDtypeStruct((n, width), BF16),
        jax.ShapeDtypeStruct((n, width), F32),
        jax.ShapeDtypeStruct((n, LANES), F32),
    )
    args = [x, w["g_mix"], w["w_qkv"], w["w_qkv"], w["w_qkv"], w["w_u"], w["w_f"], w["b_f"],
            w["g_q"], w["g_k"]]
    aliases = {}
    if kv_all is not None:
        args += list(kv_all)
        aliases = {10: 1, 11: 3}
    return pl.pallas_call(
        functools.partial(_in_proj_kernel, tw=tw, q_scale=HEAD_DIM ** -0.5 * LOG2E, n_alias=n_alias),
        grid=(n // tm, nj),
        in_specs=in_specs,
        out_specs=(tile, stacked, tile, stacked, tile, tile,
                   pl.BlockSpec((tm, LANES), lambda i, j: (i, 0))),
        out_shape=out_shape,
        scratch_shapes=[pltpu.VMEM((tm, d), BF16)],
        input_output_aliases=aliases,
        compiler_params=_params(("parallel", "arbitrary")),
        name="in_proj",
    )(*args)


def _cumsum_kernel(lf_ref, init_ref, c_ref, *, tb):
    nh, t = lf_ref.shape[1], lf_ref.shape[2]
    row = lax.broadcasted_iota(jnp.int32, (tb, tb), 0)
    col = lax.broadcasted_iota(jnp.int32, (tb, tb), 1)
    tri = jnp.where(row <= col, 1.0, 0.0).astype(BF16)
    carry = init_ref[0, :, 0:1]
    for j in range(t // tb):
        x = lf_ref[0, :, j * tb:(j + 1) * tb]
        x1 = x.astype(BF16)
        r1 = x - x1.astype(F32)
        x2 = r1.astype(BF16)
        x3 = (r1 - x2.astype(F32)).astype(BF16)
        cs = _dot(x1, tri) + _dot(x2, tri) + _dot(x3, tri) + carry
        c_ref[0, :, j * tb:(j + 1) * tb] = cs
        carry = cs[:, tb - 1:tb]


def _cumsum(lf, init):
    b, h, t = lf.shape
    return pl.pallas_call(
        functools.partial(_cumsum_kernel, tb=_tile(t, 256, LANES)),
        grid=(b,),
        in_specs=[pl.BlockSpec((1, h, t), lambda i: (i, 0, 0)),
                  pl.BlockSpec((1, h, LANES), lambda i: (i, 0, 0))],
        out_specs=pl.BlockSpec((1, h, t), lambda i: (i, 0, 0)),
        out_shape=jax.ShapeDtypeStruct((b, h, t), F32),
        compiler_params=_params(("parallel",)),
        name="cumsum",
    )(lf, init)


def _attn_kernel(q_ref, k_ref, v_ref, ck_ref, o_ref, *, tq):
    t = q_ref.shape[1]
    ck = ck_ref[0] * LOG2E
    row = lax.broadcasted_iota(jnp.int32, (tq, tq), 0)
    col = lax.broadcasted_iota(jnp.int32, (tq, tq), 1)
    for qi in range(t // tq):
        lo = qi * tq
        q = q_ref[0, lo:lo + tq, :]
        s_d = _dot_nt(q, k_ref[0, lo:lo + tq, :]) - ck[:, lo:lo + tq]
        s_d = jnp.where(col <= row, s_d, NEG_INF)
        m = jnp.max(s_d, axis=-1, keepdims=True)
        if qi > 0:
            s_o = _dot_nt(q, k_ref[0, :lo, :]) - ck[:, :lo]
            m = jnp.maximum(m, jnp.max(s_o, axis=-1, keepdims=True))
            p_o = jnp.exp2(s_o - m)
            l = jnp.sum(p_o, axis=-1, keepdims=True)
            acc = _dot(p_o.astype(BF16), v_ref[0, :lo, :])
        p_d = jnp.exp2(s_d - m)
        l_d = jnp.sum(p_d, axis=-1, keepdims=True)
        acc_d = _dot(p_d.astype(BF16), v_ref[0, lo:lo + tq, :])
        if qi > 0:
            l_d = l_d + l
            acc_d = acc_d + acc
        o_ref[0, lo:lo + tq, :] = acc_d / l_d


def _attention(q, k, v, ck, *, tq):
    b, t, width = q.shape
    nh = width // HEAD_DIM
    seq = pl.BlockSpec((1, t, HEAD_DIM), lambda bi, h: (bi, 0, h))
    return pl.pallas_call(
        functools.partial(_attn_kernel, tq=tq),
        grid=(b, nh),
        in_specs=[seq, seq, seq, pl.BlockSpec((1, 1, t), lambda bi, h: (bi * nh + h, 0, 0))],
        out_specs=seq,
        out_shape=jax.ShapeDtypeStruct((b, t, width), F32),
        compiler_params=_params(("parallel", "parallel")),
        name="attention",
    )(q, k, v, ck)


def _online_softmax_step(s, v, m_ref, l_ref, acc_ref, sl):
    m_prev = m_ref[:, sl]
    m_new = jnp.maximum(m_prev, jnp.max(s, axis=-1, keepdims=True))
    alpha = jnp.exp2(m_prev - m_new)
    p = jnp.exp2(s - m_new[:, 0:1])
    l_ref[:, sl] = alpha * l_ref[:, sl] + jnp.sum(p, axis=-1, keepdims=True)
    acc_ref[:, sl] = alpha * acc_ref[:, sl] + _dot(p.astype(BF16), v)
    m_ref[:, sl] = m_new


def _attn_cached_kernel(q_ref, kc_ref, vc_ref, ckp_ref, kn_ref, vn_ref, ckn_ref,
                        o_ref, m_ref, l_ref, acc_ref, *, nh, nkv):
    j = pl.program_id(1)

    @pl.when(j == 0)
    def _():
        m_ref[...] = jnp.full(m_ref.shape, NEG_INF, F32)
        l_ref[...] = jnp.zeros(l_ref.shape, F32)
        acc_ref[...] = jnp.zeros(acc_ref.shape, F32)

    for h in range(nh):
        sl = slice(h * HEAD_DIM, (h + 1) * HEAD_DIM)
        kh = kc_ref[0, :, sl].astype(BF16)
        vh = vc_ref[0, :, sl].astype(BF16)
        s = _dot_nt(q_ref[0, :, sl], kh) - ckp_ref[0, h:h + 1, :] * LOG2E
        _online_softmax_step(s, vh, m_ref, l_ref, acc_ref, sl)

    @pl.when(j == nkv - 1)
    def _():
        for h in range(nh):
            sl = slice(h * HEAD_DIM, (h + 1) * HEAD_DIM)
            s = _dot_nt(q_ref[0, :, sl], kn_ref[0, :, sl]) - ckn_ref[0, h:h + 1, :] * LOG2E
            row = lax.broadcasted_iota(jnp.int32, s.shape, 0)
            col = lax.broadcasted_iota(jnp.int32, s.shape, 1)
            s = jnp.where(col <= row, s, NEG_INF)
            _online_softmax_step(s, vn_ref[0, :, sl], m_ref, l_ref, acc_ref, sl)
        o_ref[0] = acc_ref[...] / l_ref[...]


def _attention_cached(q, k_cache, v_cache, ck_past, k_new, v_new, ck_new, *, tk):
    b, t, width = q.shape
    nh = width // HEAD_DIM
    past = k_cache.shape[1]
    nkv = past // tk
    per_b = lambda shape: pl.BlockSpec(shape, lambda i, j: (i, 0, 0))
    return pl.pallas_call(
        functools.partial(_attn_cached_kernel, nh=nh, nkv=nkv),
        grid=(b, nkv),
        in_specs=[
            per_b((1, t, width)),
            pl.BlockSpec((1, tk, width), lambda i, j: (i, j, 0)),
            pl.BlockSpec((1, tk, width), lambda i, j: (i, j, 0)),
            pl.BlockSpec((1, nh, tk), lambda i, j: (i, 0, j)),
            per_b((1, t, width)),
            per_b((1, t, width)),
            per_b((1, nh, t)),
        ],
        out_specs=per_b((1, t, width)),
        out_shape=jax.ShapeDtypeStruct((b, t, width), F32),
        scratch_shapes=[pltpu.VMEM((t, width), F32)] * 3,
        compiler_params=_params(("parallel", "arbitrary")),
        name="attention_cached",
    )(q, k_cache, v_cache, ck_past, k_new, v_new, ck_new)


def _on_block_diagonal(x, g, width):
    reps = LANES // x.shape[1]
    x128 = jnp.concatenate([x] * reps, axis=1) if reps > 1 else x
    wide = jnp.concatenate([x128] * (GROUPS_PER_BLOCK * x.shape[1] // LANES), axis=1)
    lane = lax.broadcasted_iota(jnp.int32, wide.shape, 1)
    lo = g * x.shape[1]
    return jnp.where((lane >= lo) & (lane < lo + x.shape[1]), wide, 0.0)


def _s5_prep_kernel(are_ref, aim_ref, ldt_ref, bre_ref, bim_ref, cre_ref, cim_ref,
                    r_ref, wst_ref, wcar_ref, a16_ref):
    L, C, P = S5_CHUNK, SSM_GROUP, SSM_STATE
    tau = lax.broadcasted_iota(jnp.int32, (L + 1, 1, 1), 0).astype(F32)
    hp = lax.Precision.HIGHEST
    a16_re, a16_im = [], []
    for g in range(GROUPS_PER_BLOCK):
        ar = are_ref[0, g:g + 1, :]
        ai = aim_ref[0, g:g + 1, :]
        dt = jnp.exp(ldt_ref[0, g:g + 1, :])
        mag = jnp.exp(tau * (ar * dt)[None])
        pw_re = mag * jnp.cos(tau * (ai * dt)[None])
        pw_im = mag * jnp.sin(tau * (ai * dt)[None])
        a16_re.append(pw_re[L])
        a16_im.append(pw_im[L])
        nr = pw_re[1] - 1.0
        ni = pw_im[1]
        den = ar * ar + ai * ai
        coef_re = (nr * ar + ni * ai) / den
        coef_im = (ni * ar - nr * ai) / den
        b_re = bre_ref[0, g]
        b_im = bim_ref[0, g]
        bb_re = coef_re * b_re - coef_im * b_im
        bb_im = coef_re * b_im + coef_im * b_re
        bp_re = pw_re * bb_re[None] - pw_im * bb_im[None]
        bp_im = pw_re * bb_im[None] + pw_im * bb_re[None]
        c_re = cre_ref[0, g][None]
        c_im = cim_ref[0, g][None]
        cp_re = c_re * pw_re - c_im * pw_im
        cp_im = c_re * pw_im + c_im * pw_re
        kt = (lax.dot_general(bp_re[:L].reshape(L * C, P), cre_ref[0, g], (((1,), (1,)), ((), ())),
                              precision=hp, preferred_element_type=F32)
              - lax.dot_general(bp_im[:L].reshape(L * C, P), cim_ref[0, g], (((1,), (1,)), ((), ())),
                                precision=hp, preferred_element_type=F32))
        for s in range(L):
            rows = slice(s * LANES + g * C, s * LANES + (g + 1) * C)
            wst_ref[0, rows, :STATE_HALF] = _on_block_diagonal(bp_re[L - 1 - s], g, P).astype(BF16)
            wst_ref[0, rows, STATE_HALF:] = _on_block_diagonal(bp_im[L - 1 - s], g, P).astype(BF16)
            wcar_ref[0, rows, :STATE_HALF] = _on_block_diagonal(cp_re[s + 1], g, P).astype(BF16)
            wcar_ref[0, rows, STATE_HALF:] = _on_block_diagonal(-cp_im[s + 1], g, P).astype(BF16)
            tau_s = L - 1 - s
            r_ref[0, rows, :] = _on_block_diagonal(kt[tau_s * C:(tau_s + 1) * C, :], g, C).astype(BF16)
    half = STATE_TILES // 2
    for j in range(half):
        a16_ref[0, j] = jnp.concatenate(a16_re[2 * j:2 * j + 2], axis=1)
        a16_ref[0, half + j] = jnp.concatenate(a16_im[2 * j:2 * j + 2], axis=1)


def _s5_prep(a_re, a_im, log_dt, b_re, b_im, c_re, c_im):
    nl, ng, ns = a_re.shape
    gpb = GROUPS_PER_BLOCK
    n = nl * ng // gpb
    blk = lambda a: a.reshape((n, gpb) + a.shape[2:])
    args = (blk(a_re), blk(a_im), blk(log_dt[..., None]),
            blk(jnp.swapaxes(b_re, -1, -2)), blk(jnp.swapaxes(b_im, -1, -2)),
            blk(c_re), blk(c_im))
    spec = lambda shape: pl.BlockSpec((1,) + shape, lambda i: (i,) + (0,) * len(shape))
    c, p = SSM_GROUP, SSM_STATE
    out_dims = [(CHUNK_W, LANES), (CHUNK_W, STATE_W), (CHUNK_W, STATE_W), (STATE_TILES, 1, LANES)]
    out_dtypes = [BF16, BF16, BF16, F32]
    r, wst, wcar, a16 = pl.pallas_call(
        _s5_prep_kernel,
        grid=(n,),
        in_specs=[spec((gpb, p)), spec((gpb, p)), spec((gpb, 1)),
                  spec((gpb, c, p)), spec((gpb, c, p)), spec((gpb, c, p)), spec((gpb, c, p))],
        out_specs=tuple(spec(d) for d in out_dims),
        out_shape=tuple(jax.ShapeDtypeStruct((n,) + d, t) for d, t in zip(out_dims, out_dtypes)),
        compiler_params=_params(("parallel",)),
        name="s5_prep",
    )(*args)
    return r, wst, wcar, a16.reshape(n * STATE_TILES, 1, LANES)


def _gather_chunk_inputs(u_ref, uflat_ref, tr):
    for s in range(S5_CHUNK):
        uflat_ref[:, s * LANES:(s + 1) * LANES] = (
            u_ref[pl.ds(s, tr, stride=S5_CHUNK), :].astype(BF16))


def _s5_state_kernel(u_ref, wst_ref, s_ref, uflat_ref, *, tr):
    _gather_chunk_inputs(u_ref, uflat_ref, tr)
    s = _dot(uflat_ref[...], wst_ref[...])
    for j in range(STATE_TILES):
        s_ref[j] = s[:, j * LANES:(j + 1) * LANES]


def _s5_state(u, wst, blk0, *, tr):
    n, width = u.shape
    rows = n // S5_CHUNK
    nblk = width // LANES
    return pl.pallas_call(
        functools.partial(_s5_state_kernel, tr=tr),
        grid=(nblk, rows // tr),
        in_specs=[pl.BlockSpec((tr * S5_CHUNK, LANES), lambda g, i: (i, g)),
                  pl.BlockSpec((None, CHUNK_W, STATE_W), lambda g, i: (blk0 + g, 0, 0))],
        out_specs=pl.BlockSpec((STATE_TILES, tr, LANES), lambda g, i: (g, i, 0)),
        out_shape=jax.ShapeDtypeStruct((nblk * STATE_TILES, rows, LANES), F32),
        scratch_shapes=[pltpu.VMEM((tr, CHUNK_W), BF16)],
        compiler_params=_params(("parallel", "parallel")),
        name="s5_state",
    )(u, wst)


def _s5_scan_kernel(s_ref, a_ref, x0_ref, xp_ref, xf_ref, *, nb, nck):
    half = STATE_TILES // 2
    a = [a_ref[j] for j in range(STATE_TILES)]

    def body(k, x):
        rows = pl.ds(k, nb, stride=nck)
        new = [None] * STATE_TILES
        for j in range(half):
            xp_ref.at[j][rows, :] = x[j]
            xp_ref.at[half + j][rows, :] = x[half + j]
            new[j] = a[j] * x[j] - a[half + j] * x[half + j] + s_ref.at[j][rows, :]
            new[half + j] = a[j] * x[half + j] + a[half + j] * x[j] + s_ref.at[half + j][rows, :]
        return tuple(new)

    x = lax.fori_loop(0, nck, body, tuple(x0_ref[j] for j in range(STATE_TILES)))
    for j in range(STATE_TILES):
        xf_ref[j] = x[j]


def _s5_scan(s, a16, x0, blk0, *, nb):
    tiles, rows, _ = s.shape
    nblk = tiles // STATE_TILES
    nck = rows // nb
    blk = lambda r: pl.BlockSpec((STATE_TILES, r, LANES), lambda g: (g, 0, 0))
    return pl.pallas_call(
        functools.partial(_s5_scan_kernel, nb=nb, nck=nck),
        grid=(nblk,),
        in_specs=[blk(rows),
                  pl.BlockSpec((STATE_TILES, 1, LANES), lambda g: (blk0 + g, 0, 0)),
                  blk(nb)],
        out_specs=(blk(rows), blk(nb)),
        out_shape=(jax.ShapeDtypeStruct((tiles, rows, LANES), F32),
                   jax.ShapeDtypeStruct((tiles, nb, LANES), F32)),
        compiler_params=_params(("parallel",)),
        name="s5_scan",
    )(s, a16, x0)


def _s5_out_kernel(u_ref, xp_ref, r_ref, wcar_ref, d_ref, g_ref, uflat_ref, *, tr):
    _gather_chunk_inputs(u_ref, uflat_ref, tr)
    x_prev = jnp.concatenate([xp_ref[j].astype(BF16) for j in range(STATE_TILES)], axis=1)
    carried = _dot_nt(x_prev, wcar_ref[...])
    d = d_ref[...]
    for t in range(S5_CHUNK):
        rows = pl.ds(t, tr, stride=S5_CHUNK)
        y = (carried[:, t * LANES:(t + 1) * LANES]
             + _dot(uflat_ref[:, :(t + 1) * LANES], r_ref[(S5_CHUNK - 1 - t) * LANES:, :])
             + d * u_ref[rows, :])
        g_ref[rows, :] = _gelu_tanh(y)


def _s5_out(l, u, x_prev, r, wcar, d_skip, blk0, *, tr):
    n, width = u.shape
    rows = n // S5_CHUNK
    nblk = width // LANES
    return pl.pallas_call(
        functools.partial(_s5_out_kernel, tr=tr),
        grid=(nblk, rows // tr),
        in_specs=[pl.BlockSpec((tr * S5_CHUNK, LANES), lambda g, i: (i, g)),
                  pl.BlockSpec((STATE_TILES, tr, LANES), lambda g, i: (g, i, 0)),
                  pl.BlockSpec((None, CHUNK_W, LANES), lambda g, i: (blk0 + g, 0, 0)),
                  pl.BlockSpec((None, CHUNK_W, STATE_W), lambda g, i: (blk0 + g, 0, 0)),
                  pl.BlockSpec((None, 1, LANES), lambda g, i: (l, 0, g))],
        out_specs=pl.BlockSpec((tr * S5_CHUNK, LANES), lambda g, i: (i, g)),
        out_shape=jax.ShapeDtypeStruct((n, width), F32),
        scratch_shapes=[pltpu.VMEM((tr, CHUNK_W), BF16)],
        compiler_params=_params(("parallel", "parallel")),
        name="s5_out",
    )(u, x_prev, r, wcar, d_skip)


def _mix_out_kernel(attn_ref, g_ref, x_ref, wglu_ref, bglu_ref, gat_ref, gss_ref,
                    wo_a_ref, wo_s_ref, h_ref):
    g = g_ref[...]
    gate = _sigmoid(_dot(g.astype(BF16), wglu_ref[...]) + bglu_ref[...])
    ssm_n = _rms(g * gate, gss_ref[...]).astype(BF16)
    attn_n = _rms(attn_ref[...], gat_ref[...]).astype(BF16)
    h_ref[...] = x_ref[...] + _dot(attn_n, wo_a_ref[...]) + _dot(ssm_n, wo_s_ref[...])


def _mix_out(l, attn, g, x, w, *, tm):
    n, d = x.shape
    wa = attn.shape[1]
    ws = g.shape[1]
    assert wa == ws
    return pl.pallas_call(
        _mix_out_kernel,
        grid=(n // tm,),
        in_specs=[pl.BlockSpec((tm, wa), lambda i: (i, 0)),
                  pl.BlockSpec((tm, ws), lambda i: (i, 0)),
                  pl.BlockSpec((tm, d), lambda i: (i, 0)),
                  pl.BlockSpec((None, ws, ws), lambda i: (l, 0, 0)),
                  _layer_vec(ws)(l), _layer_vec(wa)(l), _layer_vec(ws)(l),
                  pl.BlockSpec((None, wa, d), lambda i: (l, 0, 0)),
                  pl.BlockSpec((None, ws, d), lambda i: (l, 1, 0))],
        out_specs=pl.BlockSpec((tm, d), lambda i: (i, 0)),
        out_shape=jax.ShapeDtypeStruct((n, d), F32),
        compiler_params=_params(("parallel",)),
        name="mix_out",
    )(attn, g, x, w["w_glu"], w["b_glu"], w["g_attn"], w["g_ssm"], w["w_out"], w["w_out"])


def _ffn_kernel(h_ref, g_ref, wg_ref, wu_ref, wd_ref, o_ref, fn_ref, acc_ref, *, nf):
    f = pl.program_id(1)

    @pl.when(f == 0)
    def _():
        fn_ref[...] = _rms(h_ref[...], g_ref[...]).astype(BF16)
        acc_ref[...] = h_ref[...]

    fn = fn_ref[...]
    gate = _dot(fn, wg_ref[...])
    up = _dot(fn, wu_ref[...])
    act = (gate * _sigmoid(gate) * up).astype(BF16)
    acc_ref[...] += _dot(act, wd_ref[...])

    @pl.when(f == nf - 1)
    def _():
        o_ref[...] = acc_ref[...]


def _ffn(l, h, w, *, tm, tf):
    n, d = h.shape
    dff = w["w_gate"].shape[2]
    nf = dff // tf
    return pl.pallas_call(
        functools.partial(_ffn_kernel, nf=nf),
        grid=(n // tm, nf),
        in_specs=[pl.BlockSpec((tm, d), lambda i, f: (i, 0)),
                  _layer_vec(d)(l),
                  pl.BlockSpec((None, d, tf), lambda i, f: (l, 0, f)),
                  pl.BlockSpec((None, d, tf), lambda i, f: (l, 0, f)),
                  pl.BlockSpec((None, tf, d), lambda i, f: (l, f, 0))],
        out_specs=pl.BlockSpec((tm, d), lambda i, f: (i, 0)),
        out_shape=jax.ShapeDtypeStruct((n, d), F32),
        scratch_shapes=[pltpu.VMEM((tm, d), BF16), pltpu.VMEM((tm, d), F32)],
        compiler_params=_params(("parallel", "arbitrary")),
        name="ffn",
    )(h, w["g_ffn"], w["w_gate"], w["w_up"], w["w_down"])


def _ple_kernel(h_ref, p_ref, g_ref, wg_ref, wp_ref, o_ref):
    h = h_ref[...]
    gate = _sigmoid(_dot(_rms(h, g_ref[...]).astype(BF16), wg_ref[...]))
    o_ref[...] = h + gate * _dot(p_ref[...].astype(BF16), wp_ref[...])


def _ple(l, h, p, w, *, tm):
    n, d = h.shape
    dp = p.shape[1]
    return pl.pallas_call(
        _ple_kernel,
        grid=(n // tm,),
        in_specs=[pl.BlockSpec((tm, d), lambda i: (i, 0)),
                  pl.BlockSpec((tm, dp), lambda i: (i, 0)),
                  _layer_vec(d)(l),
                  pl.BlockSpec((None, d, d), lambda i: (l, 0, 0)),
                  pl.BlockSpec((None, dp, d), lambda i: (l, 0, 0))],
        out_specs=pl.BlockSpec((tm, d), lambda i: (i, 0)),
        out_shape=jax.ShapeDtypeStruct((n, d), F32),
        compiler_params=_params(("parallel",)),
        name="ple",
    )(h, p, w["g_ple"], w["w_ple_gate"], w["w_ple_proj"])


def _layer(l, x, p_l, cache, state0, w, s5, nh, kv_all):
    b, t, d = x.shape
    n = b * t
    wa = nh * HEAD_DIM
    xf = x.reshape(n, d)
    q, k_all, kb, v_all, vb, u, lf = _in_proj(l, xf, w, kv_all, tm=_tile(n, 1024))
    logf = lf[:, :nh].reshape(b, t, nh)
    lf_rows = jnp.swapaxes(logf, 1, 2)
    zero_init = jnp.zeros((b, nh, LANES), F32)
    seq = lambda a: a.reshape(b, t, wa)
    if cache is None:
        ck = _cumsum(lf_rows, zero_init)
        attn = _attention(seq(q), seq(kb), seq(vb), ck.reshape(b * nh, 1, t), tq=_tile(t, 512))
    else:
        k_past, v_past, lf_past = cache
        past = k_past.shape[1]
        ck_past = _cumsum(jnp.swapaxes(lf_past, 1, 2), zero_init)
        init = jnp.broadcast_to(ck_past[:, :, past - 1:past], (b, nh, LANES))
        ck_new = _cumsum(jnp.pad(lf_rows, ((0, 0), (0, 0), (0, LANES - t))), init)
        attn = _attention_cached(seq(q), k_past.reshape(b, past, wa), v_past.reshape(b, past, wa),
                                 ck_past, seq(kb), seq(vb), ck_new[:, :, :t], tk=_tile(past, 512))
    attn = attn.reshape(n, wa)

    r, wst, wcar, a16 = s5
    nblk = u.shape[1] // LANES
    rows = n // S5_CHUNK
    tr = _tile(rows, 256)
    s_contrib = _s5_state(u, wst, l * nblk, tr=tr)
    x_prev, x_fin = _s5_scan(s_contrib, a16, state0, l * nblk, nb=b)
    g = _s5_out(l, u, x_prev, r, wcar, w["d_skip"], l * nblk, tr=tr)

    h = _mix_out(l, attn, g, xf, w, tm=_tile(n, 256))
    h = _ffn(l, h, w, tm=_tile(n, 512), tf=_tile(w["w_gate"].shape[2], 512, LANES))
    h = _ple(l, h, p_l.reshape(n, -1), w, tm=_tile(n, 512))

    xs = x_fin.reshape(nblk, 2, STATE_TILES // 2, b, LANES).transpose(3, 0, 1, 2, 4)
    xs = xs.reshape(b, nblk, 2, GROUPS_PER_BLOCK, SSM_STATE)
    s_re = xs[:, :, 0].reshape(b, nblk * GROUPS_PER_BLOCK, SSM_STATE)
    s_im = xs[:, :, 1].reshape(b, nblk * GROUPS_PER_BLOCK, SSM_STATE)
    return h.reshape(b, t, d), (k_all, v_all), logf, s_re, s_im


def _pack_state(s_re, s_im):
    b, g, p = s_re.shape
    nblk = g // GROUPS_PER_BLOCK
    half = STATE_TILES // 2
    both = jnp.stack([s_re.reshape(b, nblk, half, LANES), s_im.reshape(b, nblk, half, LANES)], axis=2)
    return both.transpose(1, 2, 3, 0, 4).reshape(nblk * STATE_TILES, b, LANES)


def kernel(x_prompt, x_sample, cache_k, cache_v, cache_logf, state_ssm_re, state_ssm_im, p_prompt, p_sample, g_mix, w_in, b_f, g_q, g_k, a_re, a_im, log_dt, b_re, b_im, c_re, c_im, d_skip, w_glu, b_glu, g_attn_out, g_ssm_out, w_out, g_ffn, w_gate, w_up, w_down, g_ple, w_ple_gate, w_ple_proj):
    depth = w_in.shape[0]
    nh = b_f.shape[1]
    wa = nh * HEAD_DIM
    ws = d_skip.shape[1]
    assert wa == ws, "q/k/v/u column tiles are walked together assuming equal group widths"
    nblk = ws // LANES
    vec = lambda a: a[:, None, :]
    w = dict(
        g_mix=vec(g_mix), g_q=vec(g_q), g_k=vec(g_k), d_skip=vec(d_skip), b_glu=vec(b_glu),
        g_attn=vec(g_attn_out), g_ssm=vec(g_ssm_out), g_ffn=vec(g_ffn), g_ple=vec(g_ple),
        b_f=vec(jnp.pad(b_f, ((0, 0), (0, LANES - nh)))),
        w_qkv=_cast_bf16(w_in, cols=3 * wa),
        w_u=_cast_bf16(w_in[:, :, 3 * wa + nh:]),
        w_f=jnp.pad(w_in[:, :, 3 * wa:3 * wa + nh], ((0, 0), (0, 0), (0, LANES - nh))).astype(BF16),
        w_glu=_cast_bf16(w_glu), w_out=_cast_bf16(w_out), w_gate=_cast_bf16(w_gate),
        w_up=_cast_bf16(w_up), w_down=_cast_bf16(w_down), w_ple_gate=_cast_bf16(w_ple_gate),
        w_ple_proj=_cast_bf16(w_ple_proj))
    s5 = _s5_prep(a_re, a_im, log_dt, b_re, b_im, c_re, c_im)
    zero_state = jnp.zeros((nblk * STATE_TILES, x_prompt.shape[0], LANES), F32)
    hp, hs = x_prompt, x_sample
    kv_p = kv_s = None
    lf_p, re_p, im_p, lf_s, re_s, im_s = [], [], [], [], [], []
    for l in range(depth):
        hp, kv_p, lf, s_re, s_im = _layer(l, hp, p_prompt[l], None, zero_state, w, s5, nh, kv_p)
        lf_p.append(lf); re_p.append(s_re); im_p.append(s_im)
        hs, kv_s, lf, s_re, s_im = _layer(l, hs, p_sample[l], (cache_k[l], cache_v[l], cache_logf[l]),
                                          _pack_state(state_ssm_re[l], state_ssm_im[l]), w, s5, nh, kv_s)
        lf_s.append(lf); re_s.append(s_re); im_s.append(s_im)
    bp, tp = x_prompt.shape[:2]
    bs, ts = x_sample.shape[:2]
    kv5 = lambda a, b, t: a.reshape(depth, b, t, nh, HEAD_DIM)
    return (hp, hs,
            kv5(kv_p[0], bp, tp), kv5(kv_p[1], bp, tp), jnp.stack(lf_p), jnp.stack(re_p), jnp.stack(im_p),
            kv5(kv_s[0], bs, ts), kv5(kv_s[1], bs, ts), jnp.stack(lf_s), jnp.stack(re_s), jnp.stack(im_s))
```

```python
import functools
import math

import jax
import jax.numpy as jnp
from jax import lax
from jax.experimental import pallas as pl
from jax.experimental.pallas import tpu as pltpu

F32 = jnp.float32
BF16 = jnp.bfloat16

EPS = 1e-6
NEG_INF = -1e30
LOG2E = math.log2(math.e)
LANES = 128
HEAD_DIM = 128
SSM_GROUP = 16
SSM_STATE = 64
S5_CHUNK = 16
GROUPS_PER_BLOCK = LANES // SSM_GROUP
STATE_HALF = GROUPS_PER_BLOCK * SSM_STATE
STATE_W = 2 * STATE_HALF
STATE_TILES = STATE_W // LANES
CHUNK_W = S5_CHUNK * LANES
VMEM_LIMIT = 56 * 1024 * 1024


def _params(sem):
    return pltpu.CompilerParams(dimension_semantics=sem, vmem_limit_bytes=VMEM_LIMIT)


def _tile(n, pref, step=8):
    if n <= pref:
        return n
    t = pref // step * step
    while n % t:
        t -= step
    return t


def _rms(x, g):
    return x * lax.rsqrt(jnp.mean(x * x, axis=-1, keepdims=True) + EPS) * g


def _log_sigmoid(x):
    return jnp.minimum(x, 0.0) - jnp.log1p(jnp.exp(-jnp.abs(x)))


def _sigmoid(x):
    return 1.0 / (1.0 + jnp.exp(-x))


def _gelu_tanh(x):
    c = math.sqrt(2.0 / math.pi)
    return x * (0.5 * (1.0 + jnp.tanh(c * (x + 0.044715 * (x * x * x)))))


def _dot(a, b):
    return jnp.dot(a, b, preferred_element_type=F32)


def _dot_nt(a, b):
    return lax.dot_general(a, b, (((1,), (1,)), ((), ())), preferred_element_type=F32)


def _layer_vec(width):
    return lambda l: pl.BlockSpec((None, 1, width), lambda *_: (l, 0, 0))


def _cast_kernel(x_ref, o_ref):
    o_ref[...] = x_ref[...].astype(BF16)


def _cast_bf16(w, cols=None):
    nl, r, c = w.shape
    cols = cols or c
    tr = _tile(r, 512)
    tc = _tile(cols, 2048, LANES)
    spec = pl.BlockSpec((None, tr, tc), lambda l, i, j: (l, i, j))
    return pl.pallas_call(
        _cast_kernel,
        grid=(nl, r // tr, cols // tc),
        in_specs=[spec],
        out_specs=spec,
        out_shape=jax.ShapeDtypeStruct((nl, r, cols), BF16),
        compiler_params=_params(("parallel", "parallel", "parallel")),
        name="cast_bf16",
    )(w)


def _in_proj_kernel(*refs, tw, q_scale, n_alias):
    (x_ref, g_ref, wq_ref, wk_ref, wv_ref, wu_ref, wf_ref, bf_ref, gq_ref, gk_ref) = refs[:10]
    (q_ref, k32_ref, kb_ref, v32_ref, vb_ref, u_ref, lf_ref, xn_ref) = refs[10 + n_alias:]
    j = pl.program_id(1)

    @pl.when(j == 0)
    def _():
        xn_ref[...] = _rms(x_ref[...], g_ref[...]).astype(BF16)
        lf_ref[...] = _log_sigmoid(_dot(xn_ref[...], wf_ref[...]) + bf_ref[...])

    xn = xn_ref[...]
    q = _dot(xn, wq_ref[...])
    k = _dot(xn, wk_ref[...])
    gq = gq_ref[...] * q_scale
    gk = gk_ref[...]
    for h in range(tw // HEAD_DIM):
        sl = slice(h * HEAD_DIM, (h + 1) * HEAD_DIM)
        qh = q[:, sl]
        q_ref[:, sl] = (qh * lax.rsqrt(jnp.mean(qh * qh, axis=-1, keepdims=True) + EPS) * gq).astype(BF16)
        kh = k[:, sl]
        kn = kh * lax.rsqrt(jnp.mean(kh * kh, axis=-1, keepdims=True) + EPS) * gk
        k32_ref[:, sl] = kn
        kb_ref[:, sl] = kn.astype(BF16)
    v = _dot(xn, wv_ref[...])
    v32_ref[...] = v
    vb_ref[...] = v.astype(BF16)
    u_ref[...] = _dot(xn, wu_ref[...])


def _in_proj(l, x, w, kv_all, *, tm):
    n, d = x.shape
    nl = w["w_qkv"].shape[0]
    width = w["w_u"].shape[2]
    tw = min(2 * HEAD_DIM, width)
    nj = width // tw
    tile = pl.BlockSpec((tm, tw), lambda i, j: (i, j))
    stacked = pl.BlockSpec((None, tm, tw), lambda i, j: (l, i, j))
    wcol = lambda off: pl.BlockSpec((None, d, tw), lambda i, j: (l, 0, off * nj + j))
    n_alias = 0 if kv_all is None else 2
    in_specs = [
        pl.BlockSpec((tm, d), lambda i, j: (i, 0)),
        _layer_vec(d)(l),
        wcol(0), wcol(1), wcol(2), wcol(0),
        pl.BlockSpec((None, d, LANES), lambda i, j: (l, 0, 0)),
        _layer_vec(LANES)(l), _layer_vec(HEAD_DIM)(l), _layer_vec(HEAD_DIM)(l),
    ] + [pl.BlockSpec(memory_space=pl.ANY)] * n_alias
    out_shape = (
        jax.ShapeDtypeStruct((n, width), BF16),
        jax.ShapeDtypeStruct((nl, n, width), F32),
        jax.ShapeDtypeStruct((n, width), BF16),
        jax.ShapeDtypeStruct((nl, n, width), F32),
        jax.ShapeDtypeStruct((n, width), BF16),
        jax.ShapeDtypeStruct((n, width), F32),
        jax.ShapeDtypeStruct((n, LANES), F32),
    )
    args = [x, w["g_mix"], w["w_qkv"], w["w_qkv"], w["w_qkv"], w["w_u"], w["w_f"], w["b_f"],
            w["g_q"], w["g_k"]]
    aliases = {}
    if kv_all is not None:
        args += list(kv_all)
        aliases = {10: 1, 11: 3}
    return pl.pallas_call(
        functools.partial(_in_proj_kernel, tw=tw, q_scale=HEAD_DIM ** -0.5 * LOG2E, n_alias=n_alias),
        grid=(n // tm, nj),
        in_specs=in_specs,
        out_specs=(tile, stacked, tile, stacked, tile, tile,
                   pl.BlockSpec((tm, LANES), lambda i, j: (i, 0))),
        out_shape=out_shape,
        scratch_shapes=[pltpu.VMEM((tm, d), BF16)],
        input_output_aliases=aliases,
        compiler_params=_params(("parallel", "arbitrary")),
        name="in_proj",
    )(*args)


def _cumsum_kernel(lf_ref, init_ref, c_ref, *, tb):
    nh, t = lf_ref.shape[1], lf_ref.shape[2]
    row = lax.broadcasted_iota(jnp.int32, (tb, tb), 0)
    col = lax.broadcasted_iota(jnp.int32, (tb, tb), 1)
    tri = jnp.where(row <= col, 1.0, 0.0).astype(BF16)
    carry = init_ref[0, :, 0:1]
    for j in range(t // tb):
        x = lf_ref[0, :, j * tb:(j + 1) * tb]
        x1 = x.astype(BF16)
        r1 = x - x1.astype(F32)
        x2 = r1.astype(BF16)
        x3 = (r1 - x2.astype(F32)).astype(BF16)
        cs = _dot(x1, tri) + _dot(x2, tri) + _dot(x3, tri) + carry
        c_ref[0, :, j * tb:(j + 1) * tb] = cs
        carry = cs[:, tb - 1:tb]


def _cumsum(lf, init):
    b, h, t = lf.shape
    return pl.pallas_call(
        functools.partial(_cumsum_kernel, tb=_tile(t, 256, LANES)),
        grid=(b,),
        in_specs=[pl.BlockSpec((1, h, t), lambda i: (i, 0, 0)),
                  pl.BlockSpec((1, h, LANES), lambda i: (i, 0, 0))],
        out_specs=pl.BlockSpec((1, h, t), lambda i: (i, 0, 0)),
        out_shape=jax.ShapeDtypeStruct((b, h, t), F32),
        compiler_params=_params(("parallel",)),
        name="cumsum",
    )(lf, init)


def _attn_kernel(q_ref, k_ref, v_ref, ck_ref, o_ref, *, tq):
    t = q_ref.shape[1]
    ck = ck_ref[0] * LOG2E
    row = lax.broadcasted_iota(jnp.int32, (tq, tq), 0)
    col = lax.broadcasted_iota(jnp.int32, (tq, tq), 1)
    for qi in range(t // tq):
        lo = qi * tq
        q = q_ref[0, lo:lo + tq, :]
        s_d = _dot_nt(q, k_ref[0, lo:lo + tq, :]) - ck[:, lo:lo + tq]
        s_d = jnp.where(col <= row, s_d, NEG_INF)
        m = jnp.max(s_d, axis=-1, keepdims=True)
        if qi > 0:
            s_o = _dot_nt(q, k_ref[0, :lo, :]) - ck[:, :lo]
            m = jnp.maximum(m, jnp.max(s_o, axis=-1, keepdims=True))
            p_o = jnp.exp2(s_o - m)
            l = jnp.sum(p_o, axis=-1, keepdims=True)
            acc = _dot(p_o.astype(BF16), v_ref[0, :lo, :])
        p_d = jnp.exp2(s_d - m)
        l_d = jnp.sum(p_d, axis=-1, keepdims=True)
        acc_d = _dot(p_d.astype(BF16), v_ref[0, lo:lo + tq, :])
        if qi > 0:
            l_d = l_d + l
            acc_d = acc_d + acc
        o_ref[0, lo:lo + tq, :] = acc_d / l_d


def _attention(q, k, v, ck, *, tq):
    b, t, width = q.shape
    nh = width // HEAD_DIM
    seq = pl.BlockSpec((1, t, HEAD_DIM), lambda bi, h: (bi, 0, h))
    return pl.pallas_call(
        functools.partial(_attn_kernel, tq=tq),
        grid=(b, nh),
        in_specs=[seq, seq, seq, pl.BlockSpec((1, 1, t), lambda bi, h: (bi * nh + h, 0, 0))],
        out_specs=seq,
        out_shape=jax.ShapeDtypeStruct((b, t, width), F32),
        compiler_params=_params(("parallel", "parallel")),
        name="attention",
    )(q, k, v, ck)


def _online_softmax_step(s, v, m_ref, l_ref, acc_ref, sl):
    m_prev = m_ref[:, sl]
    m_new = jnp.maximum(m_prev, jnp.max(s, axis=-1, keepdims=True))
    alpha = jnp.exp2(m_prev - m_new)
    p = jnp.exp2(s - m_new[:, 0:1])
    l_ref[:, sl] = alpha * l_ref[:, sl] + jnp.sum(p, axis=-1, keepdims=True)
    acc_ref[:, sl] = alpha * acc_ref[:, sl] + _dot(p.astype(BF16), v)
    m_ref[:, sl] = m_new


def _attn_cached_kernel(q_ref, kc_ref, vc_ref, ckp_ref, kn_ref, vn_ref, ckn_ref,
                        o_ref, m_ref, l_ref, acc_ref, *, nh, nkv):
    j = pl.program_id(1)

    @pl.when(j == 0)
    def _():
        m_ref[...] = jnp.full(m_ref.shape, NEG_INF, F32)
        l_ref[...] = jnp.zeros(l_ref.shape, F32)
        acc_ref[...] = jnp.zeros(acc_ref.shape, F32)

    for h in range(nh):
        sl = slice(h * HEAD_DIM, (h + 1) * HEAD_DIM)
        kh = kc_ref[:, sl].astype(BF16)
        vh = vc_ref[:, sl].astype(BF16)
        s = _dot_nt(q_ref[0, :, sl], kh) - ckp_ref[0, h:h + 1, :] * LOG2E
        _online_softmax_step(s, vh, m_ref, l_ref, acc_ref, sl)

    @pl.when(j == nkv - 1)
    def _():
        for h in range(nh):
            sl = slice(h * HEAD_DIM, (h + 1) * HEAD_DIM)
            s = _dot_nt(q_ref[0, :, sl], kn_ref[0, :, sl]) - ckn_ref[0, h:h + 1, :] * LOG2E
            row = lax.broadcasted_iota(jnp.int32, s.shape, 0)
            col = lax.broadcasted_iota(jnp.int32, s.shape, 1)
            s = jnp.where(col <= row, s, NEG_INF)
            _online_softmax_step(s, vn_ref[0, :, sl], m_ref, l_ref, acc_ref, sl)
        o_ref[0] = acc_ref[...] / l_ref[...]


def _attention_cached(l, q, k_cache, v_cache, ck_past, k_new, v_new, ck_new, *, tk):
    b, t, width = q.shape
    nh = width // HEAD_DIM
    past = k_cache.shape[2]
    nkv = past // tk
    per_b = lambda shape: pl.BlockSpec(shape, lambda i, j: (i, 0, 0))
    return pl.pallas_call(
        functools.partial(_attn_cached_kernel, nh=nh, nkv=nkv),
        grid=(b, nkv),
        in_specs=[
            per_b((1, t, width)),
            pl.BlockSpec((None, None, tk, width), lambda i, j: (l, i, j, 0)),
            pl.BlockSpec((None, None, tk, width), lambda i, j: (l, i, j, 0)),
            pl.BlockSpec((1, nh, tk), lambda i, j: (i, 0, j)),
            per_b((1, t, width)),
            per_b((1, t, width)),
            per_b((1, nh, t)),
        ],
        out_specs=per_b((1, t, width)),
        out_shape=jax.ShapeDtypeStruct((b, t, width), F32),
        scratch_shapes=[pltpu.VMEM((t, width), F32)] * 3,
        compiler_params=_params(("parallel", "arbitrary")),
        name="attention_cached",
    )(q, k_cache, v_cache, ck_past, k_new, v_new, ck_new)


def _on_block_diagonal(x, g, width):
    reps = LANES // x.shape[1]
    x128 = jnp.concatenate([x] * reps, axis=1) if reps > 1 else x
    wide = jnp.concatenate([x128] * (GROUPS_PER_BLOCK * x.shape[1] // LANES), axis=1)
    lane = lax.broadcasted_iota(jnp.int32, wide.shape, 1)
    lo = g * x.shape[1]
    return jnp.where((lane >= lo) & (lane < lo + x.shape[1]), wide, 0.0)


def _s5_prep_kernel(are_ref, aim_ref, ldt_ref, bre_ref, bim_ref, cre_ref, cim_ref,
                    r_ref, wst_ref, wcar_ref, a16_ref):
    L, C, P = S5_CHUNK, SSM_GROUP, SSM_STATE
    tau = lax.broadcasted_iota(jnp.int32, (L + 1, 1, 1), 0).astype(F32)
    hp = lax.Precision.HIGHEST
    a16_re, a16_im = [], []
    for g in range(GROUPS_PER_BLOCK):
        ar = are_ref[0, g:g + 1, :]
        ai = aim_ref[0, g:g + 1, :]
        dt = jnp.exp(ldt_ref[0, g:g + 1, :])
        mag = jnp.exp(tau * (ar * dt)[None])
        pw_re = mag * jnp.cos(tau * (ai * dt)[None])
        pw_im = mag * jnp.sin(tau * (ai * dt)[None])
        a16_re.append(pw_re[L])
        a16_im.append(pw_im[L])
        nr = pw_re[1] - 1.0
        ni = pw_im[1]
        den = ar * ar + ai * ai
        coef_re = (nr * ar + ni * ai) / den
        coef_im = (ni * ar - nr * ai) / den
        b_re = bre_ref[0, g]
        b_im = bim_ref[0, g]
        bb_re = coef_re * b_re - coef_im * b_im
        bb_im = coef_re * b_im + coef_im * b_re
        bp_re = pw_re * bb_re[None] - pw_im * bb_im[None]
        bp_im = pw_re * bb_im[None] + pw_im * bb_re[None]
        c_re = cre_ref[0, g][None]
        c_im = cim_ref[0, g][None]
        cp_re = c_re * pw_re - c_im * pw_im
        cp_im = c_re * pw_im + c_im * pw_re
        kt = (lax.dot_general(bp_re[:L].reshape(L * C, P), cre_ref[0, g], (((1,), (1,)), ((), ())),
                              precision=hp, preferred_element_type=F32)
              - lax.dot_general(bp_im[:L].reshape(L * C, P), cim_ref[0, g], (((1,), (1,)), ((), ())),
                                precision=hp, preferred_element_type=F32))
        for s in range(L):
            rows = slice(s * LANES + g * C, s * LANES + (g + 1) * C)
            wst_ref[0, rows, :STATE_HALF] = _on_block_diagonal(bp_re[L - 1 - s], g, P).astype(BF16)
            wst_ref[0, rows, STATE_HALF:] = _on_block_diagonal(bp_im[L - 1 - s], g, P).astype(BF16)
            wcar_ref[0, rows, :STATE_HALF] = _on_block_diagonal(cp_re[s + 1], g, P).astype(BF16)
            wcar_ref[0, rows, STATE_HALF:] = _on_block_diagonal(-cp_im[s + 1], g, P).astype(BF16)
            tau_s = L - 1 - s
            r_ref[0, rows, :] = _on_block_diagonal(kt[tau_s * C:(tau_s + 1) * C, :], g, C).astype(BF16)
    half = STATE_TILES // 2
    for j in range(half):
        a16_ref[0, j] = jnp.concatenate(a16_re[2 * j:2 * j + 2], axis=1)
        a16_ref[0, half + j] = jnp.concatenate(a16_im[2 * j:2 * j + 2], axis=1)


def _s5_prep(a_re, a_im, log_dt, b_re, b_im, c_re, c_im):
    nl, ng, ns = a_re.shape
    gpb = GROUPS_PER_BLOCK
    n = nl * ng // gpb
    blk = lambda a: a.reshape((n, gpb) + a.shape[2:])
    args = (blk(a_re), blk(a_im), blk(log_dt[..., None]),
            blk(jnp.swapaxes(b_re, -1, -2)), blk(jnp.swapaxes(b_im, -1, -2)),
            blk(c_re), blk(c_im))
    spec = lambda shape: pl.BlockSpec((1,) + shape, lambda i: (i,) + (0,) * len(shape))
    c, p = SSM_GROUP, SSM_STATE
    out_dims = [(CHUNK_W, LANES), (CHUNK_W, STATE_W), (CHUNK_W, STATE_W), (STATE_TILES, 1, LANES)]
    out_dtypes = [BF16, BF16, BF16, F32]
    r, wst, wcar, a16 = pl.pallas_call(
        _s5_prep_kernel,
        grid=(n,),
        in_specs=[spec((gpb, p)), spec((gpb, p)), spec((gpb, 1)),
                  spec((gpb, c, p)), spec((gpb, c, p)), spec((gpb, c, p)), spec((gpb, c, p))],
        out_specs=tuple(spec(d) for d in out_dims),
        out_shape=tuple(jax.ShapeDtypeStruct((n,) + d, t) for d, t in zip(out_dims, out_dtypes)),
        compiler_params=_params(("parallel",)),
        name="s5_prep",
    )(*args)
    return r, wst, wcar, a16.reshape(n * STATE_TILES, 1, LANES)


def _gather_chunk_inputs(u_ref, uflat_ref, tr):
    for s in range(S5_CHUNK):
        uflat_ref[:, s * LANES:(s + 1) * LANES] = (
            u_ref[pl.ds(s, tr, stride=S5_CHUNK), :].astype(BF16))


def _s5_state_kernel(u_ref, wst_ref, s_ref, uflat_ref, *, tr):
    _gather_chunk_inputs(u_ref, uflat_ref, tr)
    s = _dot(uflat_ref[...], wst_ref[...])
    for j in range(STATE_TILES):
        s_ref[j] = s[:, j * LANES:(j + 1) * LANES]


def _s5_state(u, wst, blk0, *, tr):
    n, width = u.shape
    rows = n // S5_CHUNK
    nblk = width // LANES
    return pl.pallas_call(
        functools.partial(_s5_state_kernel, tr=tr),
        grid=(nblk, rows // tr),
        in_specs=[pl.BlockSpec((tr * S5_CHUNK, LANES), lambda g, i: (i, g)),
                  pl.BlockSpec((None, CHUNK_W, STATE_W), lambda g, i: (blk0 + g, 0, 0))],
        out_specs=pl.BlockSpec((STATE_TILES, tr, LANES), lambda g, i: (g, i, 0)),
        out_shape=jax.ShapeDtypeStruct((nblk * STATE_TILES, rows, LANES), F32),
        scratch_shapes=[pltpu.VMEM((tr, CHUNK_W), BF16)],
        compiler_params=_params(("parallel", "parallel")),
        name="s5_state",
    )(u, wst)


def _s5_scan_kernel(s_ref, a_ref, x0_ref, xp_ref, xf_ref, *, nb, nck):
    half = STATE_TILES // 2
    a = [a_ref[j] for j in range(STATE_TILES)]

    def body(k, x):
        rows = pl.ds(k, nb, stride=nck)
        new = [None] * STATE_TILES
        for j in range(half):
            xp_ref.at[j][rows, :] = x[j]
            xp_ref.at[half + j][rows, :] = x[half + j]
            new[j] = a[j] * x[j] - a[half + j] * x[half + j] + s_ref.at[j][rows, :]
            new[half + j] = a[j] * x[half + j] + a[half + j] * x[j] + s_ref.at[half + j][rows, :]
        return tuple(new)

    x = lax.fori_loop(0, nck, body, tuple(x0_ref[j] for j in range(STATE_TILES)))
    for j in range(STATE_TILES):
        xf_ref[j] = x[j]


def _s5_scan(s, a16, x0, blk0, *, nb):
    tiles, rows, _ = s.shape
    nblk = tiles // STATE_TILES
    nck = rows // nb
    blk = lambda r: pl.BlockSpec((STATE_TILES, r, LANES), lambda g: (g, 0, 0))
    return pl.pallas_call(
        functools.partial(_s5_scan_kernel, nb=nb, nck=nck),
        grid=(nblk,),
        in_specs=[blk(rows),
                  pl.BlockSpec((STATE_TILES, 1, LANES), lambda g: (blk0 + g, 0, 0)),
                  blk(nb)],
        out_specs=(blk(rows), blk(nb)),
        out_shape=(jax.ShapeDtypeStruct((tiles, rows, LANES), F32),
                   jax.ShapeDtypeStruct((tiles, nb, LANES), F32)),
        compiler_params=_params(("parallel",)),
        name="s5_scan",
    )(s, a16, x0)


def _s5_out_kernel(u_ref, xp_ref, r_ref, wcar_ref, d_ref, g_ref, uflat_ref, *, tr):
    _gather_chunk_inputs(u_ref, uflat_ref, tr)
    x_prev = jnp.concatenate([xp_ref[j].astype(BF16) for j in range(STATE_TILES)], axis=1)
    carried = _dot_nt(x_prev, wcar_ref[...])
    d = d_ref[...]
    for t in range(S5_CHUNK):
        rows = pl.ds(t, tr, stride=S5_CHUNK)
        y = (carried[:, t * LANES:(t + 1) * LANES]
             + _dot(uflat_ref[:, :(t + 1) * LANES], r_ref[(S5_CHUNK - 1 - t) * LANES:, :])
             + d * u_ref[rows, :])
        g_ref[rows, :] = _gelu_tanh(y)


def _s5_out(l, u, x_prev, r, wcar, d_skip, blk0, *, tr):
    n, width = u.shape
    rows = n // S5_CHUNK
    nblk = width // LANES
    return pl.pallas_call(
        functools.partial(_s5_out_kernel, tr=tr),
        grid=(nblk, rows // tr),
        in_specs=[pl.BlockSpec((tr * S5_CHUNK, LANES), lambda g, i: (i, g)),
                  pl.BlockSpec((STATE_TILES, tr, LANES), lambda g, i: (g, i, 0)),
                  pl.BlockSpec((None, CHUNK_W, LANES), lambda g, i: (blk0 + g, 0, 0)),
                  pl.BlockSpec((None, CHUNK_W, STATE_W), lambda g, i: (blk0 + g, 0, 0)),
                  pl.BlockSpec((None, 1, LANES), lambda g, i: (l, 0, g))],
        out_specs=pl.BlockSpec((tr * S5_CHUNK, LANES), lambda g, i: (i, g)),
        out_shape=jax.ShapeDtypeStruct((n, width), F32),
        scratch_shapes=[pltpu.VMEM((tr, CHUNK_W), BF16)],
        compiler_params=_params(("parallel", "parallel")),
        name="s5_out",
    )(u, x_prev, r, wcar, d_skip)


def _mix_out_kernel(attn_ref, g_ref, x_ref, wglu_ref, bglu_ref, gat_ref, gss_ref,
                    wo_a_ref, wo_s_ref, h_ref):
    g = g_ref[...]
    gate = _sigmoid(_dot(g.astype(BF16), wglu_ref[...]) + bglu_ref[...])
    ssm_n = _rms(g * gate, gss_ref[...]).astype(BF16)
    attn_n = _rms(attn_ref[...], gat_ref[...]).astype(BF16)
    h_ref[...] = x_ref[...] + _dot(attn_n, wo_a_ref[...]) + _dot(ssm_n, wo_s_ref[...])


def _mix_out(l, attn, g, x, w, *, tm):
    n, d = x.shape
    wa = attn.shape[1]
    ws = g.shape[1]
    assert wa == ws
    return pl.pallas_call(
        _mix_out_kernel,
        grid=(n // tm,),
        in_specs=[pl.BlockSpec((tm, wa), lambda i: (i, 0)),
                  pl.BlockSpec((tm, ws), lambda i: (i, 0)),
                  pl.BlockSpec((tm, d), lambda i: (i, 0)),
                  pl.BlockSpec((None, ws, ws), lambda i: (l, 0, 0)),
                  _layer_vec(ws)(l), _layer_vec(wa)(l), _layer_vec(ws)(l),
                  pl.BlockSpec((None, wa, d), lambda i: (l, 0, 0)),
                  pl.BlockSpec((None, ws, d), lambda i: (l, 1, 0))],
        out_specs=pl.BlockSpec((tm, d), lambda i: (i, 0)),
        out_shape=jax.ShapeDtypeStruct((n, d), F32),
        compiler_params=_params(("parallel",)),
        name="mix_out",
    )(attn, g, x, w["w_glu"], w["b_glu"], w["g_attn"], w["g_ssm"], w["w_out"], w["w_out"])


def _ffn_kernel(h_ref, g_ref, wg_ref, wu_ref, wd_ref, o_ref, fn_ref, acc_ref, *, nf):
    f = pl.program_id(1)

    @pl.when(f == 0)
    def _():
        fn_ref[...] = _rms(h_ref[...], g_ref[...]).astype(BF16)
        acc_ref[...] = h_ref[...]

    fn = fn_ref[...]
    gate = _dot(fn, wg_ref[...])
    up = _dot(fn, wu_ref[...])
    act = (gate * _sigmoid(gate) * up).astype(BF16)
    acc_ref[...] += _dot(act, wd_ref[...])

    @pl.when(f == nf - 1)
    def _():
        o_ref[...] = acc_ref[...]


def _ffn(l, h, w, *, tm, tf):
    n, d = h.shape
    dff = w["w_gate"].shape[2]
    nf = dff // tf
    return pl.pallas_call(
        functools.partial(_ffn_kernel, nf=nf),
        grid=(n // tm, nf),
        in_specs=[pl.BlockSpec((tm, d), lambda i, f: (i, 0)),
                  _layer_vec(d)(l),
                  pl.BlockSpec((None, d, tf), lambda i, f: (l, 0, f)),
                  pl.BlockSpec((None, d, tf), lambda i, f: (l, 0, f)),
                  pl.BlockSpec((None, tf, d), lambda i, f: (l, f, 0))],
        out_specs=pl.BlockSpec((tm, d), lambda i, f: (i, 0)),
        out_shape=jax.ShapeDtypeStruct((n, d), F32),
        scratch_shapes=[pltpu.VMEM((tm, d), BF16), pltpu.VMEM((tm, d), F32)],
        compiler_params=_params(("parallel", "arbitrary")),
        name="ffn",
    )(h, w["g_ffn"], w["w_gate"], w["w_up"], w["w_down"])


def _ple_kernel(h_ref, p_ref, g_ref, wg_ref, wp_ref, o_ref):
    h = h_ref[...]
    gate = _sigmoid(_dot(_rms(h, g_ref[...]).astype(BF16), wg_ref[...]))
    o_ref[...] = h + gate * _dot(p_ref[...].astype(BF16), wp_ref[...])


def _ple(l, h, p, w, *, tm):
    n, d = h.shape
    dp = p.shape[1]
    return pl.pallas_call(
        _ple_kernel,
        grid=(n // tm,),
        in_specs=[pl.BlockSpec((tm, d), lambda i: (i, 0)),
                  pl.BlockSpec((tm, dp), lambda i: (i, 0)),
                  _layer_vec(d)(l),
                  pl.BlockSpec((None, d, d), lambda i: (l, 0, 0)),
                  pl.BlockSpec((None, dp, d), lambda i: (l, 0, 0))],
        out_specs=pl.BlockSpec((tm, d), lambda i: (i, 0)),
        out_shape=jax.ShapeDtypeStruct((n, d), F32),
        compiler_params=_params(("parallel",)),
        name="ple",
    )(h, p, w["g_ple"], w["w_ple_gate"], w["w_ple_proj"])


def _layer(l, x, p_l, cache, state0, w, s5, nh, kv_all):
    b, t, d = x.shape
    n = b * t
    wa = nh * HEAD_DIM
    xf = x.reshape(n, d)
    q, k_all, kb, v_all, vb, u, lf = _in_proj(l, xf, w, kv_all, tm=_tile(n, 1024))
    logf = lf[:, :nh].reshape(b, t, nh)
    lf_rows = jnp.swapaxes(logf, 1, 2)
    zero_init = jnp.zeros((b, nh, LANES), F32)
    seq = lambda a: a.reshape(b, t, wa)
    if cache is None:
        ck = _cumsum(lf_rows, zero_init)
        attn = _attention(seq(q), seq(kb), seq(vb), ck.reshape(b * nh, 1, t), tq=_tile(t, 512))
    else:
        k_past, v_past, lf_past = cache
        nl, _, past = k_past.shape[:3]
        ck_past = _cumsum(jnp.swapaxes(lf_past, 1, 2), zero_init)
        init = jnp.broadcast_to(ck_past[:, :, past - 1:past], (b, nh, LANES))
        ck_new = _cumsum(jnp.pad(lf_rows, ((0, 0), (0, 0), (0, LANES - t))), init)
        attn = _attention_cached(l, seq(q), k_past.reshape(nl, b, past, wa),
                                 v_past.reshape(nl, b, past, wa), ck_past, seq(kb), seq(vb),
                                 ck_new[:, :, :t], tk=_tile(past, 2048))
    attn = attn.reshape(n, wa)

    r, wst, wcar, a16 = s5
    nblk = u.shape[1] // LANES
    rows = n // S5_CHUNK
    tr = _tile(rows, 256)
    s_contrib = _s5_state(u, wst, l * nblk, tr=tr)
    x_prev, x_fin = _s5_scan(s_contrib, a16, state0, l * nblk, nb=b)
    g = _s5_out(l, u, x_prev, r, wcar, w["d_skip"], l * nblk, tr=tr)

    h = _mix_out(l, attn, g, xf, w, tm=_tile(n, 256))
    h = _ffn(l, h, w, tm=_tile(n, 512), tf=_tile(w["w_gate"].shape[2], 512, LANES))
    h = _ple(l, h, p_l.reshape(n, -1), w, tm=_tile(n, 512))

    xs = x_fin.reshape(nblk, 2, STATE_TILES // 2, b, LANES).transpose(3, 0, 1, 2, 4)
    xs = xs.reshape(b, nblk, 2, GROUPS_PER_BLOCK, SSM_STATE)
    s_re = xs[:, :, 0].reshape(b, nblk * GROUPS_PER_BLOCK, SSM_STATE)
    s_im = xs[:, :, 1].reshape(b, nblk * GROUPS_PER_BLOCK, SSM_STATE)
    return h.reshape(b, t, d), (k_all, v_all), logf, s_re, s_im


def _pack_state(s_re, s_im):
    b, g, p = s_re.shape
    nblk = g // GROUPS_PER_BLOCK
    half = STATE_TILES // 2
    both = jnp.stack([s_re.reshape(b, nblk, half, LANES), s_im.reshape(b, nblk, half, LANES)], axis=2)
    return both.transpose(1, 2, 3, 0, 4).reshape(nblk * STATE_TILES, b, LANES)


def kernel(x_prompt, x_sample, cache_k, cache_v, cache_logf, state_ssm_re, state_ssm_im, p_prompt, p_sample, g_mix, w_in, b_f, g_q, g_k, a_re, a_im, log_dt, b_re, b_im, c_re, c_im, d_skip, w_glu, b_glu, g_attn_out, g_ssm_out, w_out, g_ffn, w_gate, w_up, w_down, g_ple, w_ple_gate, w_ple_proj):
    depth = w_in.shape[0]
    nh = b_f.shape[1]
    wa = nh * HEAD_DIM
    ws = d_skip.shape[1]
    assert wa == ws, "q/k/v/u column tiles are walked together assuming equal group widths"
    nblk = ws // LANES
    vec = lambda a: a[:, None, :]
    w = dict(
        g_mix=vec(g_mix), g_q=vec(g_q), g_k=vec(g_k), d_skip=vec(d_skip), b_glu=vec(b_glu),
        g_attn=vec(g_attn_out), g_ssm=vec(g_ssm_out), g_ffn=vec(g_ffn), g_ple=vec(g_ple),
        b_f=vec(jnp.pad(b_f, ((0, 0), (0, LANES - nh)))),
        w_qkv=_cast_bf16(w_in, cols=3 * wa),
        w_u=_cast_bf16(w_in[:, :, 3 * wa + nh:]),
        w_f=jnp.pad(w_in[:, :, 3 * wa:3 * wa + nh], ((0, 0), (0, 0), (0, LANES - nh))).astype(BF16),
        w_glu=_cast_bf16(w_glu), w_out=_cast_bf16(w_out), w_gate=_cast_bf16(w_gate),
        w_up=_cast_bf16(w_up), w_down=_cast_bf16(w_down), w_ple_gate=_cast_bf16(w_ple_gate),
        w_ple_proj=_cast_bf16(w_ple_proj))
    s5 = _s5_prep(a_re, a_im, log_dt, b_re, b_im, c_re, c_im)
    zero_state = jnp.zeros((nblk * STATE_TILES, x_prompt.shape[0], LANES), F32)
    hp, hs = x_prompt, x_sample
    kv_p = kv_s = None
    lf_p, re_p, im_p, lf_s, re_s, im_s = [], [], [], [], [], []
    for l in range(depth):
        hp, kv_p, lf, s_re, s_im = _layer(l, hp, p_prompt[l], None, zero_state, w, s5, nh, kv_p)
        lf_p.append(lf); re_p.append(s_re); im_p.append(s_im)
        hs, kv_s, lf, s_re, s_im = _layer(l, hs, p_sample[l], (cache_k, cache_v, cache_logf[l]),
                                          _pack_state(state_ssm_re[l], state_ssm_im[l]), w, s5, nh, kv_s)
        lf_s.append(lf); re_s.append(s_re); im_s.append(s_im)
    bp, tp = x_prompt.shape[:2]
    bs, ts = x_sample.shape[:2]
    kv5 = lambda a, b, t: a.reshape(depth, b, t, nh, HEAD_DIM)
    return (hp, hs,
            kv5(kv_p[0], bp, tp), kv5(kv_p[1], bp, tp), jnp.stack(lf_p), jnp.stack(re_p), jnp.stack(im_p),
            kv5(kv_s[0], bs, ts), kv5(kv_s[1], bs, ts), jnp.stack(lf_s), jnp.stack(re_s), jnp.stack(im_s))
```

```python
import functools
import math

import jax
import jax.numpy as jnp
from jax import lax
from jax.experimental import pallas as pl
from jax.experimental.pallas import tpu as pltpu

F32 = jnp.float32
BF16 = jnp.bfloat16

EPS = 1e-6
NEG_INF = -1e30
LOG2E = math.log2(math.e)
LANES = 128
HEAD_DIM = 128
SSM_GROUP = 16
SSM_STATE = 64
S5_CHUNK = 16
GROUPS_PER_BLOCK = LANES // SSM_GROUP
STATE_HALF = GROUPS_PER_BLOCK * SSM_STATE
STATE_W = 2 * STATE_HALF
STATE_TILES = STATE_W // LANES
CHUNK_W = S5_CHUNK * LANES
VMEM_LIMIT = 56 * 1024 * 1024


def _params(sem):
    return pltpu.CompilerParams(dimension_semantics=sem, vmem_limit_bytes=VMEM_LIMIT)


def _tile(n, pref, step=8):
    if n <= pref:
        return n
    t = pref // step * step
    while n % t:
        t -= step
    return t


def _rms(x, g):
    return x * lax.rsqrt(jnp.mean(x * x, axis=-1, keepdims=True) + EPS) * g


def _log_sigmoid(x):
    return jnp.minimum(x, 0.0) - jnp.log1p(jnp.exp(-jnp.abs(x)))


def _sigmoid(x):
    return 1.0 / (1.0 + jnp.exp(-x))


def _gelu_tanh(x):
    c = math.sqrt(2.0 / math.pi)
    return x * (0.5 * (1.0 + jnp.tanh(c * (x + 0.044715 * (x * x * x)))))


def _dot(a, b):
    return jnp.dot(a, b, preferred_element_type=F32)


def _dot_nt(a, b):
    return lax.dot_general(a, b, (((1,), (1,)), ((), ())), preferred_element_type=F32)


def _layer_vec(width):
    return lambda l: pl.BlockSpec((None, 1, width), lambda *_: (l, 0, 0))


def _cast_kernel(x_ref, o_ref):
    o_ref[...] = x_ref[...].astype(BF16)


def _cast_bf16(w, cols=None):
    nl, r, c = w.shape
    cols = cols or c
    tr = _tile(r, 512)
    tc = _tile(cols, 2048, LANES)
    spec = pl.BlockSpec((None, tr, tc), lambda l, i, j: (l, i, j))
    return pl.pallas_call(
        _cast_kernel,
        grid=(nl, r // tr, cols // tc),
        in_specs=[spec],
        out_specs=spec,
        out_shape=jax.ShapeDtypeStruct((nl, r, cols), BF16),
        compiler_params=_params(("parallel", "parallel", "parallel")),
        name="cast_bf16",
    )(w)


def _split_heads_kernel(x_ref, o_ref):
    for h in range(o_ref.shape[1]):
        o_ref[:, h, :] = x_ref[:, h * HEAD_DIM:(h + 1) * HEAD_DIM]


def _split_heads(x, *, tm):
    nl, n, width = x.shape
    nh = width // HEAD_DIM
    return pl.pallas_call(
        _split_heads_kernel,
        grid=(nl, n // tm),
        in_specs=[pl.BlockSpec((None, tm, width), lambda l, i: (l, i, 0))],
        out_specs=pl.BlockSpec((None, tm, nh, HEAD_DIM), lambda l, i: (l, i, 0, 0)),
        out_shape=jax.ShapeDtypeStruct((nl, n, nh, HEAD_DIM), F32),
        compiler_params=_params(("parallel", "parallel")),
        name="split_heads",
    )(x)


def _in_proj_kernel(*refs, tw, q_scale, n_alias):
    (x_ref, g_ref, wq_ref, wk_ref, wv_ref, wu_ref, wf_ref, bf_ref, gq_ref, gk_ref) = refs[:10]
    (q_ref, k32_ref, kb_ref, v32_ref, vb_ref, u_ref, lf_ref, xn_ref) = refs[10 + n_alias:]
    j = pl.program_id(1)

    @pl.when(j == 0)
    def _():
        xn_ref[...] = _rms(x_ref[...], g_ref[...]).astype(BF16)
        lf_ref[...] = _log_sigmoid(_dot(xn_ref[...], wf_ref[...]) + bf_ref[...])

    xn = xn_ref[...]
    q = _dot(xn, wq_ref[...])
    k = _dot(xn, wk_ref[...])
    gq = gq_ref[...] * q_scale
    gk = gk_ref[...]
    for h in range(tw // HEAD_DIM):
        sl = slice(h * HEAD_DIM, (h + 1) * HEAD_DIM)
        qh = q[:, sl]
        q_ref[:, sl] = (qh * lax.rsqrt(jnp.mean(qh * qh, axis=-1, keepdims=True) + EPS) * gq).astype(BF16)
        kh = k[:, sl]
        kn = kh * lax.rsqrt(jnp.mean(kh * kh, axis=-1, keepdims=True) + EPS) * gk
        k32_ref[:, sl] = kn
        kb_ref[:, sl] = kn.astype(BF16)
    v = _dot(xn, wv_ref[...])
    v32_ref[...] = v
    vb_ref[...] = v.astype(BF16)
    u_ref[...] = _dot(xn, wu_ref[...])


def _in_proj(l, x, w, kv_all, *, tm):
    n, d = x.shape
    nl = w["w_qkv"].shape[0]
    width = w["w_u"].shape[2]
    tw = min(2 * HEAD_DIM, width)
    nj = width // tw
    tile = pl.BlockSpec((tm, tw), lambda i, j: (i, j))
    stacked = pl.BlockSpec((None, tm, tw), lambda i, j: (l, i, j))
    wcol = lambda off: pl.BlockSpec((None, d, tw), lambda i, j: (l, 0, off * nj + j))
    n_alias = 0 if kv_all is None else 2
    in_specs = [
        pl.BlockSpec((tm, d), lambda i, j: (i, 0)),
        _layer_vec(d)(l),
        wcol(0), wcol(1), wcol(2), wcol(0),
        pl.BlockSpec((None, d, LANES), lambda i, j: (l, 0, 0)),
        _layer_vec(LANES)(l), _layer_vec(HEAD_DIM)(l), _layer_vec(HEAD_DIM)(l),
    ] + [pl.BlockSpec(memory_space=pl.ANY)] * n_alias
    out_shape = (
        jax.ShapeDtypeStruct((n, width), BF16),
        jax.ShapeDtypeStruct((nl, n, width), F32),
        jax.ShapeDtypeStruct((n, width), BF16),
        jax.ShapeDtypeStruct((nl, n, width), F32),
        jax.ShapeDtypeStruct((n, width), BF16),
        jax.ShapeDtypeStruct((n, width), F32),
        jax.ShapeDtypeStruct((n, LANES), F32),
    )
    args = [x, w["g_mix"], w["w_qkv"], w["w_qkv"], w["w_qkv"], w["w_u"], w["w_f"], w["b_f"],
            w["g_q"], w["g_k"]]
    aliases = {}
    if kv_all is not None:
        args += list(kv_all)
        aliases = {10: 1, 11: 3}
    return pl.pallas_call(
        functools.partial(_in_proj_kernel, tw=tw, q_scale=HEAD_DIM ** -0.5 * LOG2E, n_alias=n_alias),
        grid=(n // tm, nj),
        in_specs=in_specs,
        out_specs=(tile, stacked, tile, stacked, tile, tile,
                   pl.BlockSpec((tm, LANES), lambda i, j: (i, 0))),
        out_shape=out_shape,
        scratch_shapes=[pltpu.VMEM((tm, d), BF16)],
        input_output_aliases=aliases,
        compiler_params=_params(("parallel", "arbitrary")),
        name="in_proj",
    )(*args)


def _cumsum_kernel(lf_ref, init_ref, c_ref, *, tb):
    nh, t = lf_ref.shape[1], lf_ref.shape[2]
    row = lax.broadcasted_iota(jnp.int32, (tb, tb), 0)
    col = lax.broadcasted_iota(jnp.int32, (tb, tb), 1)
    tri = jnp.where(row <= col, 1.0, 0.0).astype(BF16)
    carry = init_ref[0, :, 0:1]
    for j in range(t // tb):
        x = lf_ref[0, :, j * tb:(j + 1) * tb]
        x1 = x.astype(BF16)
        r1 = x - x1.astype(F32)
        x2 = r1.astype(BF16)
        x3 = (r1 - x2.astype(F32)).astype(BF16)
        cs = _dot(x1, tri) + _dot(x2, tri) + _dot(x3, tri) + carry
        c_ref[0, :, j * tb:(j + 1) * tb] = cs
        carry = cs[:, tb - 1:tb]


def _cumsum(lf, init):
    b, h, t = lf.shape
    return pl.pallas_call(
        functools.partial(_cumsum_kernel, tb=_tile(t, 256, LANES)),
        grid=(b,),
        in_specs=[pl.BlockSpec((1, h, t), lambda i: (i, 0, 0)),
                  pl.BlockSpec((1, h, LANES), lambda i: (i, 0, 0))],
        out_specs=pl.BlockSpec((1, h, t), lambda i: (i, 0, 0)),
        out_shape=jax.ShapeDtypeStruct((b, h, t), F32),
        compiler_params=_params(("parallel",)),
        name="cumsum",
    )(lf, init)


def _attn_kernel(q_ref, k_ref, v_ref, ck_ref, o_ref, *, tq):
    t = q_ref.shape[1]
    ck = ck_ref[0] * LOG2E
    row = lax.broadcasted_iota(jnp.int32, (tq, tq), 0)
    col = lax.broadcasted_iota(jnp.int32, (tq, tq), 1)
    for qi in range(t // tq):
        lo = qi * tq
        q = q_ref[0, lo:lo + tq, :]
        s_d = _dot_nt(q, k_ref[0, lo:lo + tq, :]) - ck[:, lo:lo + tq]
        s_d = jnp.where(col <= row, s_d, NEG_INF)
        m = jnp.max(s_d, axis=-1, keepdims=True)
        if qi > 0:
            s_o = _dot_nt(q, k_ref[0, :lo, :]) - ck[:, :lo]
            m = jnp.maximum(m, jnp.max(s_o, axis=-1, keepdims=True))
            p_o = jnp.exp2(s_o - m)
            l = jnp.sum(p_o, axis=-1, keepdims=True)
            acc = _dot(p_o.astype(BF16), v_ref[0, :lo, :])
        p_d = jnp.exp2(s_d - m)
        l_d = jnp.sum(p_d, axis=-1, keepdims=True)
        acc_d = _dot(p_d.astype(BF16), v_ref[0, lo:lo + tq, :])
        if qi > 0:
            l_d = l_d + l
            acc_d = acc_d + acc
        o_ref[0, lo:lo + tq, :] = acc_d / l_d


def _attention(q, k, v, ck, *, tq):
    b, t, width = q.shape
    nh = width // HEAD_DIM
    seq = pl.BlockSpec((1, t, HEAD_DIM), lambda bi, h: (bi, 0, h))
    return pl.pallas_call(
        functools.partial(_attn_kernel, tq=tq),
        grid=(b, nh),
        in_specs=[seq, seq, seq, pl.BlockSpec((1, 1, t), lambda bi, h: (bi * nh + h, 0, 0))],
        out_specs=seq,
        out_shape=jax.ShapeDtypeStruct((b, t, width), F32),
        compiler_params=_params(("parallel", "parallel")),
        name="attention",
    )(q, k, v, ck)


def _online_softmax_step(s, v, m_ref, l_ref, acc_ref, sl):
    m_prev = m_ref[:, sl]
    m_new = jnp.maximum(m_prev, jnp.max(s, axis=-1, keepdims=True))
    alpha = jnp.exp2(m_prev - m_new)
    p = jnp.exp2(s - m_new[:, 0:1])
    l_ref[:, sl] = alpha * l_ref[:, sl] + jnp.sum(p, axis=-1, keepdims=True)
    acc_ref[:, sl] = alpha * acc_ref[:, sl] + _dot(p.astype(BF16), v)
    m_ref[:, sl] = m_new


def _attn_cached_kernel(q_ref, kc_ref, vc_ref, ckp_ref, kn_ref, vn_ref, ckn_ref,
                        o_ref, m_ref, l_ref, acc_ref, *, nh, nkv):
    j = pl.program_id(1)

    @pl.when(j == 0)
    def _():
        m_ref[...] = jnp.full(m_ref.shape, NEG_INF, F32)
        l_ref[...] = jnp.zeros(l_ref.shape, F32)
        acc_ref[...] = jnp.zeros(acc_ref.shape, F32)

    for h in range(nh):
        sl = slice(h * HEAD_DIM, (h + 1) * HEAD_DIM)
        kh = kc_ref[:, sl].astype(BF16)
        vh = vc_ref[:, sl].astype(BF16)
        s = _dot_nt(q_ref[0, :, sl], kh) - ckp_ref[0, h:h + 1, :] * LOG2E
        _online_softmax_step(s, vh, m_ref, l_ref, acc_ref, sl)

    @pl.when(j == nkv - 1)
    def _():
        for h in range(nh):
            sl = slice(h * HEAD_DIM, (h + 1) * HEAD_DIM)
            s = _dot_nt(q_ref[0, :, sl], kn_ref[0, :, sl]) - ckn_ref[0, h:h + 1, :] * LOG2E
            row = lax.broadcasted_iota(jnp.int32, s.shape, 0)
            col = lax.broadcasted_iota(jnp.int32, s.shape, 1)
            s = jnp.where(col <= row, s, NEG_INF)
            _online_softmax_step(s, vn_ref[0, :, sl], m_ref, l_ref, acc_ref, sl)
        o_ref[0] = acc_ref[...] / l_ref[...]


def _attention_cached(l, q, k_cache, v_cache, ck_past, k_new, v_new, ck_new, *, tk):
    b, t, width = q.shape
    nh = width // HEAD_DIM
    past = k_cache.shape[2]
    nkv = past // tk
    per_b = lambda shape: pl.BlockSpec(shape, lambda i, j: (i, 0, 0))
    return pl.pallas_call(
        functools.partial(_attn_cached_kernel, nh=nh, nkv=nkv),
        grid=(b, nkv),
        in_specs=[
            per_b((1, t, width)),
            pl.BlockSpec((None, None, tk, width), lambda i, j: (l, i, j, 0)),
            pl.BlockSpec((None, None, tk, width), lambda i, j: (l, i, j, 0)),
            pl.BlockSpec((1, nh, tk), lambda i, j: (i, 0, j)),
            per_b((1, t, width)),
            per_b((1, t, width)),
            per_b((1, nh, t)),
        ],
        out_specs=per_b((1, t, width)),
        out_shape=jax.ShapeDtypeStruct((b, t, width), F32),
        scratch_shapes=[pltpu.VMEM((t, width), F32)] * 3,
        compiler_params=_params(("parallel", "arbitrary")),
        name="attention_cached",
    )(q, k_cache, v_cache, ck_past, k_new, v_new, ck_new)


def _on_block_diagonal(x, g, width):
    reps = LANES // x.shape[1]
    x128 = jnp.concatenate([x] * reps, axis=1) if reps > 1 else x
    wide = jnp.concatenate([x128] * (GROUPS_PER_BLOCK * x.shape[1] // LANES), axis=1)
    lane = lax.broadcasted_iota(jnp.int32, wide.shape, 1)
    lo = g * x.shape[1]
    return jnp.where((lane >= lo) & (lane < lo + x.shape[1]), wide, 0.0)


def _s5_prep_kernel(are_ref, aim_ref, ldt_ref, bre_ref, bim_ref, cre_ref, cim_ref,
                    r_ref, wst_ref, wcar_ref, a16_ref):
    L, C, P = S5_CHUNK, SSM_GROUP, SSM_STATE
    tau = lax.broadcasted_iota(jnp.int32, (L + 1, 1, 1), 0).astype(F32)
    hp = lax.Precision.HIGHEST
    a16_re, a16_im = [], []
    for g in range(GROUPS_PER_BLOCK):
        ar = are_ref[0, g:g + 1, :]
        ai = aim_ref[0, g:g + 1, :]
        dt = jnp.exp(ldt_ref[0, g:g + 1, :])
        mag = jnp.exp(tau * (ar * dt)[None])
        pw_re = mag * jnp.cos(tau * (ai * dt)[None])
        pw_im = mag * jnp.sin(tau * (ai * dt)[None])
        a16_re.append(pw_re[L])
        a16_im.append(pw_im[L])
        nr = pw_re[1] - 1.0
        ni = pw_im[1]
        den = ar * ar + ai * ai
        coef_re = (nr * ar + ni * ai) / den
        coef_im = (ni * ar - nr * ai) / den
        b_re = bre_ref[0, g]
        b_im = bim_ref[0, g]
        bb_re = coef_re * b_re - coef_im * b_im
        bb_im = coef_re * b_im + coef_im * b_re
        bp_re = pw_re * bb_re[None] - pw_im * bb_im[None]
        bp_im = pw_re * bb_im[None] + pw_im * bb_re[None]
        c_re = cre_ref[0, g][None]
        c_im = cim_ref[0, g][None]
        cp_re = c_re * pw_re - c_im * pw_im
        cp_im = c_re * pw_im + c_im * pw_re
        kt = (lax.dot_general(bp_re[:L].reshape(L * C, P), cre_ref[0, g], (((1,), (1,)), ((), ())),
                              precision=hp, preferred_element_type=F32)
              - lax.dot_general(bp_im[:L].reshape(L * C, P), cim_ref[0, g], (((1,), (1,)), ((), ())),
                                precision=hp, preferred_element_type=F32))
        for s in range(L):
            rows = slice(s * LANES + g * C, s * LANES + (g + 1) * C)
            wst_ref[0, rows, :STATE_HALF] = _on_block_diagonal(bp_re[L - 1 - s], g, P).astype(BF16)
            wst_ref[0, rows, STATE_HALF:] = _on_block_diagonal(bp_im[L - 1 - s], g, P).astype(BF16)
            wcar_ref[0, rows, :STATE_HALF] = _on_block_diagonal(cp_re[s + 1], g, P).astype(BF16)
            wcar_ref[0, rows, STATE_HALF:] = _on_block_diagonal(-cp_im[s + 1], g, P).astype(BF16)
            tau_s = L - 1 - s
            r_ref[0, rows, :] = _on_block_diagonal(kt[tau_s * C:(tau_s + 1) * C, :], g, C).astype(BF16)
    half = STATE_TILES // 2
    for j in range(half):
        a16_ref[0, j] = jnp.concatenate(a16_re[2 * j:2 * j + 2], axis=1)
        a16_ref[0, half + j] = jnp.concatenate(a16_im[2 * j:2 * j + 2], axis=1)


def _s5_prep(a_re, a_im, log_dt, b_re, b_im, c_re, c_im):
    nl, ng, ns = a_re.shape
    gpb = GROUPS_PER_BLOCK
    n = nl * ng // gpb
    blk = lambda a: a.reshape((n, gpb) + a.shape[2:])
    args = (blk(a_re), blk(a_im), blk(log_dt[..., None]),
            blk(jnp.swapaxes(b_re, -1, -2)), blk(jnp.swapaxes(b_im, -1, -2)),
            blk(c_re), blk(c_im))
    spec = lambda shape: pl.BlockSpec((1,) + shape, lambda i: (i,) + (0,) * len(shape))
    c, p = SSM_GROUP, SSM_STATE
    out_dims = [(CHUNK_W, LANES), (CHUNK_W, STATE_W), (CHUNK_W, STATE_W), (STATE_TILES, 1, LANES)]
    out_dtypes = [BF16, BF16, BF16, F32]
    r, wst, wcar, a16 = pl.pallas_call(
        _s5_prep_kernel,
        grid=(n,),
        in_specs=[spec((gpb, p)), spec((gpb, p)), spec((gpb, 1)),
                  spec((gpb, c, p)), spec((gpb, c, p)), spec((gpb, c, p)), spec((gpb, c, p))],
        out_specs=tuple(spec(d) for d in out_dims),
        out_shape=tuple(jax.ShapeDtypeStruct((n,) + d, t) for d, t in zip(out_dims, out_dtypes)),
        compiler_params=_params(("parallel",)),
        name="s5_prep",
    )(*args)
    return r, wst, wcar, a16.reshape(n * STATE_TILES, 1, LANES)


def _gather_chunk_inputs(u_ref, uflat_ref, tr):
    for s in range(S5_CHUNK):
        uflat_ref[:, s * LANES:(s + 1) * LANES] = (
            u_ref[pl.ds(s, tr, stride=S5_CHUNK), :].astype(BF16))


def _s5_state_kernel(u_ref, wst_ref, s_ref, uflat_ref, *, tr):
    _gather_chunk_inputs(u_ref, uflat_ref, tr)
    s = _dot(uflat_ref[...], wst_ref[...])
    for j in range(STATE_TILES):
        s_ref[j] = s[:, j * LANES:(j + 1) * LANES]


def _s5_state(u, wst, blk0, *, tr):
    n, width = u.shape
    rows = n // S5_CHUNK
    nblk = width // LANES
    return pl.pallas_call(
        functools.partial(_s5_state_kernel, tr=tr),
        grid=(nblk, rows // tr),
        in_specs=[pl.BlockSpec((tr * S5_CHUNK, LANES), lambda g, i: (i, g)),
                  pl.BlockSpec((None, CHUNK_W, STATE_W), lambda g, i: (blk0 + g, 0, 0))],
        out_specs=pl.BlockSpec((STATE_TILES, tr, LANES), lambda g, i: (g, i, 0)),
        out_shape=jax.ShapeDtypeStruct((nblk * STATE_TILES, rows, LANES), F32),
        scratch_shapes=[pltpu.VMEM((tr, CHUNK_W), BF16)],
        compiler_params=_params(("parallel", "parallel")),
        name="s5_state",
    )(u, wst)


def _s5_scan_kernel(s_ref, a_ref, x0_ref, xp_ref, xf_ref, *, nb, nck):
    half = STATE_TILES // 2
    a = [a_ref[j] for j in range(STATE_TILES)]

    def body(k, x):
        rows = pl.ds(k, nb, stride=nck)
        new = [None] * STATE_TILES
        for j in range(half):
            xp_ref.at[j][rows, :] = x[j]
            xp_ref.at[half + j][rows, :] = x[half + j]
            new[j] = a[j] * x[j] - a[half + j] * x[half + j] + s_ref.at[j][rows, :]
            new[half + j] = a[j] * x[half + j] + a[half + j] * x[j] + s_ref.at[half + j][rows, :]
        return tuple(new)

    x = lax.fori_loop(0, nck, body, tuple(x0_ref[j] for j in range(STATE_TILES)))
    for j in range(STATE_TILES):
        xf_ref[j] = x[j]


def _s5_scan(s, a16, x0, blk0, *, nb):
    tiles, rows, _ = s.shape
    nblk = tiles // STATE_TILES
    nck = rows // nb
    blk = lambda r: pl.BlockSpec((STATE_TILES, r, LANES), lambda g: (g, 0, 0))
    return pl.pallas_call(
        functools.partial(_s5_scan_kernel, nb=nb, nck=nck),
        grid=(nblk,),
        in_specs=[blk(rows),
                  pl.BlockSpec((STATE_TILES, 1, LANES), lambda g: (blk0 + g, 0, 0)),
                  blk(nb)],
        out_specs=(blk(rows), blk(nb)),
        out_shape=(jax.ShapeDtypeStruct((tiles, rows, LANES), F32),
                   jax.ShapeDtypeStruct((tiles, nb, LANES), F32)),
        compiler_params=_params(("parallel",)),
        name="s5_scan",
    )(s, a16, x0)


def _s5_out_kernel(u_ref, xp_ref, r_ref, wcar_ref, d_ref, g_ref, uflat_ref, *, tr):
    _gather_chunk_inputs(u_ref, uflat_ref, tr)
    x_prev = jnp.concatenate([xp_ref[j].astype(BF16) for j in range(STATE_TILES)], axis=1)
    carried = _dot_nt(x_prev, wcar_ref[...])
    d = d_ref[...]
    for t in range(S5_CHUNK):
        rows = pl.ds(t, tr, stride=S5_CHUNK)
        y = (carried[:, t * LANES:(t + 1) * LANES]
             + _dot(uflat_ref[:, :(t + 1) * LANES], r_ref[(S5_CHUNK - 1 - t) * LANES:, :])
             + d * u_ref[rows, :])
        g_ref[rows, :] = _gelu_tanh(y)


def _s5_out(l, u, x_prev, r, wcar, d_skip, blk0, *, tr):
    n, width = u.shape
    rows = n // S5_CHUNK
    nblk = width // LANES
    return pl.pallas_call(
        functools.partial(_s5_out_kernel, tr=tr),
        grid=(nblk, rows // tr),
        in_specs=[pl.BlockSpec((tr * S5_CHUNK, LANES), lambda g, i: (i, g)),
                  pl.BlockSpec((STATE_TILES, tr, LANES), lambda g, i: (g, i, 0)),
                  pl.BlockSpec((None, CHUNK_W, LANES), lambda g, i: (blk0 + g, 0, 0)),
                  pl.BlockSpec((None, CHUNK_W, STATE_W), lambda g, i: (blk0 + g, 0, 0)),
                  pl.BlockSpec((None, 1, LANES), lambda g, i: (l, 0, g))],
        out_specs=pl.BlockSpec((tr * S5_CHUNK, LANES), lambda g, i: (i, g)),
        out_shape=jax.ShapeDtypeStruct((n, width), F32),
        scratch_shapes=[pltpu.VMEM((tr, CHUNK_W), BF16)],
        compiler_params=_params(("parallel", "parallel")),
        name="s5_out",
    )(u, x_prev, r, wcar, d_skip)


def _mix_out_kernel(attn_ref, g_ref, x_ref, wglu_ref, bglu_ref, gat_ref, gss_ref,
                    wo_a_ref, wo_s_ref, h_ref):
    g = g_ref[...]
    gate = _sigmoid(_dot(g.astype(BF16), wglu_ref[...]) + bglu_ref[...])
    ssm_n = _rms(g * gate, gss_ref[...]).astype(BF16)
    attn_n = _rms(attn_ref[...], gat_ref[...]).astype(BF16)
    h_ref[...] = x_ref[...] + _dot(attn_n, wo_a_ref[...]) + _dot(ssm_n, wo_s_ref[...])


def _mix_out(l, attn, g, x, w, *, tm):
    n, d = x.shape
    wa = attn.shape[1]
    ws = g.shape[1]
    assert wa == ws
    return pl.pallas_call(
        _mix_out_kernel,
        grid=(n // tm,),
        in_specs=[pl.BlockSpec((tm, wa), lambda i: (i, 0)),
                  pl.BlockSpec((tm, ws), lambda i: (i, 0)),
                  pl.BlockSpec((tm, d), lambda i: (i, 0)),
                  pl.BlockSpec((None, ws, ws), lambda i: (l, 0, 0)),
                  _layer_vec(ws)(l), _layer_vec(wa)(l), _layer_vec(ws)(l),
                  pl.BlockSpec((None, wa, d), lambda i: (l, 0, 0)),
                  pl.BlockSpec((None, ws, d), lambda i: (l, 1, 0))],
        out_specs=pl.BlockSpec((tm, d), lambda i: (i, 0)),
        out_shape=jax.ShapeDtypeStruct((n, d), F32),
        compiler_params=_params(("parallel",)),
        name="mix_out",
    )(attn, g, x, w["w_glu"], w["b_glu"], w["g_attn"], w["g_ssm"], w["w_out"], w["w_out"])


def _ffn_kernel(h_ref, g_ref, wg_ref, wu_ref, wd_ref, o_ref, fn_ref, acc_ref, *, nf):
    f = pl.program_id(1)

    @pl.when(f == 0)
    def _():
        fn_ref[...] = _rms(h_ref[...], g_ref[...]).astype(BF16)
        acc_ref[...] = h_ref[...]

    fn = fn_ref[...]
    gate = _dot(fn, wg_ref[...])
    up = _dot(fn, wu_ref[...])
    act = (gate * _sigmoid(gate) * up).astype(BF16)
    acc_ref[...] += _dot(act, wd_ref[...])

    @pl.when(f == nf - 1)
    def _():
        o_ref[...] = acc_ref[...]


def _ffn(l, h, w, *, tm, tf):
    n, d = h.shape
    dff = w["w_gate"].shape[2]
    nf = dff // tf
    return pl.pallas_call(
        functools.partial(_ffn_kernel, nf=nf),
        grid=(n // tm, nf),
        in_specs=[pl.BlockSpec((tm, d), lambda i, f: (i, 0)),
                  _layer_vec(d)(l),
                  pl.BlockSpec((None, d, tf), lambda i, f: (l, 0, f)),
                  pl.BlockSpec((None, d, tf), lambda i, f: (l, 0, f)),
                  pl.BlockSpec((None, tf, d), lambda i, f: (l, f, 0))],
        out_specs=pl.BlockSpec((tm, d), lambda i, f: (i, 0)),
        out_shape=jax.ShapeDtypeStruct((n, d), F32),
        scratch_shapes=[pltpu.VMEM((tm, d), BF16), pltpu.VMEM((tm, d), F32)],
        compiler_params=_params(("parallel", "arbitrary")),
        name="ffn",
    )(h, w["g_ffn"], w["w_gate"], w["w_up"], w["w_down"])


def _ple_kernel(h_ref, p_ref, g_ref, wg_ref, wp_ref, o_ref):
    h = h_ref[...]
    gate = _sigmoid(_dot(_rms(h, g_ref[...]).astype(BF16), wg_ref[...]))
    o_ref[...] = h + gate * _dot(p_ref[...].astype(BF16), wp_ref[...])


def _ple(l, h, p, w, *, tm):
    n, d = h.shape
    dp = p.shape[1]
    return pl.pallas_call(
        _ple_kernel,
        grid=(n // tm,),
        in_specs=[pl.BlockSpec((tm, d), lambda i: (i, 0)),
                  pl.BlockSpec((tm, dp), lambda i: (i, 0)),
                  _layer_vec(d)(l),
                  pl.BlockSpec((None, d, d), lambda i: (l, 0, 0)),
                  pl.BlockSpec((None, dp, d), lambda i: (l, 0, 0))],
        out_specs=pl.BlockSpec((tm, d), lambda i: (i, 0)),
        out_shape=jax.ShapeDtypeStruct((n, d), F32),
        compiler_params=_params(("parallel",)),
        name="ple",
    )(h, p, w["g_ple"], w["w_ple_gate"], w["w_ple_proj"])


def _layer(l, x, p_l, cache, state0, w, s5, nh, kv_all):
    b, t, d = x.shape
    n = b * t
    wa = nh * HEAD_DIM
    xf = x.reshape(n, d)
    q, k_all, kb, v_all, vb, u, lf = _in_proj(l, xf, w, kv_all, tm=_tile(n, 1024))
    logf = lf[:, :nh].reshape(b, t, nh)
    lf_rows = jnp.swapaxes(logf, 1, 2)
    zero_init = jnp.zeros((b, nh, LANES), F32)
    seq = lambda a: a.reshape(b, t, wa)
    if cache is None:
        ck = _cumsum(lf_rows, zero_init)
        attn = _attention(seq(q), seq(kb), seq(vb), ck.reshape(b * nh, 1, t), tq=_tile(t, 512))
    else:
        k_past, v_past, lf_past = cache
        nl, _, past = k_past.shape[:3]
        ck_past = _cumsum(jnp.swapaxes(lf_past, 1, 2), zero_init)
        init = jnp.broadcast_to(ck_past[:, :, past - 1:past], (b, nh, LANES))
        ck_new = _cumsum(jnp.pad(lf_rows, ((0, 0), (0, 0), (0, LANES - t))), init)
        attn = _attention_cached(l, seq(q), k_past.reshape(nl, b, past, wa),
                                 v_past.reshape(nl, b, past, wa), ck_past, seq(kb), seq(vb),
                                 ck_new[:, :, :t], tk=_tile(past, 2048))
    attn = attn.reshape(n, wa)

    r, wst, wcar, a16 = s5
    nblk = u.shape[1] // LANES
    rows = n // S5_CHUNK
    tr = _tile(rows, 256)
    s_contrib = _s5_state(u, wst, l * nblk, tr=tr)
    x_prev, x_fin = _s5_scan(s_contrib, a16, state0, l * nblk, nb=b)
    g = _s5_out(l, u, x_prev, r, wcar, w["d_skip"], l * nblk, tr=tr)

    h = _mix_out(l, attn, g, xf, w, tm=_tile(n, 256))
    h = _ffn(l, h, w, tm=_tile(n, 512), tf=_tile(w["w_gate"].shape[2], 512, LANES))
    h = _ple(l, h, p_l.reshape(n, -1), w, tm=_tile(n, 512))

    xs = x_fin.reshape(nblk, 2, STATE_TILES // 2, b, LANES).transpose(3, 0, 1, 2, 4)
    xs = xs.reshape(b, nblk, 2, GROUPS_PER_BLOCK, SSM_STATE)
    s_re = xs[:, :, 0].reshape(b, nblk * GROUPS_PER_BLOCK, SSM_STATE)
    s_im = xs[:, :, 1].reshape(b, nblk * GROUPS_PER_BLOCK, SSM_STATE)
    return h.reshape(b, t, d), (k_all, v_all), logf, s_re, s_im


def _pack_state(s_re, s_im):
    b, g, p = s_re.shape
    nblk = g // GROUPS_PER_BLOCK
    half = STATE_TILES // 2
    both = jnp.stack([s_re.reshape(b, nblk, half, LANES), s_im.reshape(b, nblk, half, LANES)], axis=2)
    return both.transpose(1, 2, 3, 0, 4).reshape(nblk * STATE_TILES, b, LANES)


def kernel(x_prompt, x_sample, cache_k, cache_v, cache_logf, state_ssm_re, state_ssm_im, p_prompt, p_sample, g_mix, w_in, b_f, g_q, g_k, a_re, a_im, log_dt, b_re, b_im, c_re, c_im, d_skip, w_glu, b_glu, g_attn_out, g_ssm_out, w_out, g_ffn, w_gate, w_up, w_down, g_ple, w_ple_gate, w_ple_proj):
    depth = w_in.shape[0]
    nh = b_f.shape[1]
    wa = nh * HEAD_DIM
    ws = d_skip.shape[1]
    assert wa == ws, "q/k/v/u column tiles are walked together assuming equal group widths"
    nblk = ws // LANES
    vec = lambda a: a[:, None, :]
    w = dict(
        g_mix=vec(g_mix), g_q=vec(g_q), g_k=vec(g_k), d_skip=vec(d_skip), b_glu=vec(b_glu),
        g_attn=vec(g_attn_out), g_ssm=vec(g_ssm_out), g_ffn=vec(g_ffn), g_ple=vec(g_ple),
        b_f=vec(jnp.pad(b_f, ((0, 0), (0, LANES - nh)))),
        w_qkv=_cast_bf16(w_in, cols=3 * wa),
        w_u=_cast_bf16(w_in[:, :, 3 * wa + nh:]),
        w_f=jnp.pad(w_in[:, :, 3 * wa:3 * wa + nh], ((0, 0), (0, 0), (0, LANES - nh))).astype(BF16),
        w_glu=_cast_bf16(w_glu), w_out=_cast_bf16(w_out), w_gate=_cast_bf16(w_gate),
        w_up=_cast_bf16(w_up), w_down=_cast_bf16(w_down), w_ple_gate=_cast_bf16(w_ple_gate),
        w_ple_proj=_cast_bf16(w_ple_proj))
    s5 = _s5_prep(a_re, a_im, log_dt, b_re, b_im, c_re, c_im)
    zero_state = jnp.zeros((nblk * STATE_TILES, x_prompt.shape[0], LANES), F32)
    hp, hs = x_prompt, x_sample
    kv_p = kv_s = None
    lf_p, re_p, im_p, lf_s, re_s, im_s = [], [], [], [], [], []
    for l in range(depth):
        hp, kv_p, lf, s_re, s_im = _layer(l, hp, p_prompt[l], None, zero_state, w, s5, nh, kv_p)
        lf_p.append(lf); re_p.append(s_re); im_p.append(s_im)
        hs, kv_s, lf, s_re, s_im = _layer(l, hs, p_sample[l], (cache_k, cache_v, cache_logf[l]),
                                          _pack_state(state_ssm_re[l], state_ssm_im[l]), w, s5, nh, kv_s)
        lf_s.append(lf); re_s.append(s_re); im_s.append(s_im)
    bp, tp = x_prompt.shape[:2]
    bs, ts = x_sample.shape[:2]
    kv5 = lambda a, b, t: a.reshape(depth, b, t, nh, HEAD_DIM)
    heads = lambda a: _split_heads(a, tm=_tile(bp * tp, 1024))
    return (hp, hs,
            kv5(heads(kv_p[0]), bp, tp), kv5(heads(kv_p[1]), bp, tp), jnp.stack(lf_p), jnp.stack(re_p), jnp.stack(im_p),
            kv5(kv_s[0], bs, ts), kv5(kv_s[1], bs, ts), jnp.stack(lf_s), jnp.stack(re_s), jnp.stack(im_s))
```

```python
import functools
import math

import jax
import jax.numpy as jnp
from jax import lax
from jax.experimental import pallas as pl
from jax.experimental.pallas import tpu as pltpu

F32 = jnp.float32
BF16 = jnp.bfloat16

EPS = 1e-6
NEG_INF = -1e30
LOG2E = math.log2(math.e)
LANES = 128
HEAD_DIM = 128
SSM_GROUP = 16
SSM_STATE = 64
S5_CHUNK = 16
GROUPS_PER_BLOCK = LANES // SSM_GROUP
STATE_HALF = GROUPS_PER_BLOCK * SSM_STATE
STATE_W = 2 * STATE_HALF
STATE_TILES = STATE_W // LANES
CHUNK_W = S5_CHUNK * LANES
VMEM_LIMIT = 56 * 1024 * 1024


def _params(sem):
    return pltpu.CompilerParams(dimension_semantics=sem, vmem_limit_bytes=VMEM_LIMIT)


def _tile(n, pref, step=8):
    if n <= pref:
        return n
    t = pref // step * step
    while n % t:
        t -= step
    return t


def _rms(x, g):
    return x * lax.rsqrt(jnp.mean(x * x, axis=-1, keepdims=True) + EPS) * g


def _log_sigmoid(x):
    return jnp.minimum(x, 0.0) - jnp.log1p(jnp.exp(-jnp.abs(x)))


def _sigmoid(x):
    return 1.0 / (1.0 + jnp.exp(-x))


def _gelu_tanh(x):
    c = math.sqrt(2.0 / math.pi)
    return x * (0.5 * (1.0 + jnp.tanh(c * (x + 0.044715 * (x * x * x)))))


def _dot(a, b):
    return jnp.dot(a, b, preferred_element_type=F32)


def _dot_nt(a, b):
    return lax.dot_general(a, b, (((1,), (1,)), ((), ())), preferred_element_type=F32)


def _layer_vec(width):
    return lambda l: pl.BlockSpec((None, 1, width), lambda *_: (l, 0, 0))


def _cast_kernel(x_ref, o_ref):
    o_ref[...] = x_ref[...].astype(BF16)


def _cast_bf16(w, cols=None):
    nl, r, c = w.shape
    cols = cols or c
    tr = _tile(r, 512)
    tc = _tile(cols, 2048, LANES)
    spec = pl.BlockSpec((None, tr, tc), lambda l, i, j: (l, i, j))
    return pl.pallas_call(
        _cast_kernel,
        grid=(nl, r // tr, cols // tc),
        in_specs=[spec],
        out_specs=spec,
        out_shape=jax.ShapeDtypeStruct((nl, r, cols), BF16),
        compiler_params=_params(("parallel", "parallel", "parallel")),
        name="cast_bf16",
    )(w)


def _in_proj_kernel(*refs, tw, q_scale, n_alias):
    (x_ref, g_ref, wq_ref, wk_ref, wv_ref, wu_ref, wf_ref, bf_ref, gq_ref, gk_ref) = refs[:10]
    (q_ref, k32_ref, kb_ref, v32_ref, vb_ref, u_ref, lf_ref, xn_ref) = refs[10 + n_alias:]
    j = pl.program_id(1)

    @pl.when(j == 0)
    def _():
        xn_ref[...] = _rms(x_ref[...], g_ref[...]).astype(BF16)
        lf_ref[...] = _log_sigmoid(_dot(xn_ref[...], wf_ref[...]) + bf_ref[...])

    xn = xn_ref[...]
    q = _dot(xn, wq_ref[...])
    k = _dot(xn, wk_ref[...])
    gq = gq_ref[...] * q_scale
    gk = gk_ref[...]
    for h in range(tw // HEAD_DIM):
        sl = slice(h * HEAD_DIM, (h + 1) * HEAD_DIM)
        qh = q[:, sl]
        q_ref[:, sl] = (qh * lax.rsqrt(jnp.mean(qh * qh, axis=-1, keepdims=True) + EPS) * gq).astype(BF16)
        kh = k[:, sl]
        kn = kh * lax.rsqrt(jnp.mean(kh * kh, axis=-1, keepdims=True) + EPS) * gk
        k32_ref[:, sl] = kn
        kb_ref[:, sl] = kn.astype(BF16)
    v = _dot(xn, wv_ref[...])
    v32_ref[...] = v
    vb_ref[...] = v.astype(BF16)
    u_ref[...] = _dot(xn, wu_ref[...])


def _in_proj(l, x, w, kv_all, *, tm):
    n, d = x.shape
    nl = w["w_qkv"].shape[0]
    width = w["w_u"].shape[2]
    tw = min(2 * HEAD_DIM, width)
    nj = width // tw
    tile = pl.BlockSpec((tm, tw), lambda i, j: (i, j))
    stacked = pl.BlockSpec((None, tm, tw), lambda i, j: (l, i, j))
    wcol = lambda off: pl.BlockSpec((None, d, tw), lambda i, j: (l, 0, off * nj + j))
    n_alias = 0 if kv_all is None else 2
    in_specs = [
        pl.BlockSpec((tm, d), lambda i, j: (i, 0)),
        _layer_vec(d)(l),
        wcol(0), wcol(1), wcol(2), wcol(0),
        pl.BlockSpec((None, d, LANES), lambda i, j: (l, 0, 0)),
        _layer_vec(LANES)(l), _layer_vec(HEAD_DIM)(l), _layer_vec(HEAD_DIM)(l),
    ] + [pl.BlockSpec(memory_space=pl.ANY)] * n_alias
    out_shape = (
        jax.ShapeDtypeStruct((n, width), BF16),
        jax.ShapeDtypeStruct((nl, n, width), F32),
        jax.ShapeDtypeStruct((n, width), BF16),
        jax.ShapeDtypeStruct((nl, n, width), F32),
        jax.ShapeDtypeStruct((n, width), BF16),
        jax.ShapeDtypeStruct((n, width), F32),
        jax.ShapeDtypeStruct((n, LANES), F32),
    )
    args = [x, w["g_mix"], w["w_qkv"], w["w_qkv"], w["w_qkv"], w["w_u"], w["w_f"], w["b_f"],
            w["g_q"], w["g_k"]]
    aliases = {}
    if kv_all is not None:
        args += list(kv_all)
        aliases = {10: 1, 11: 3}
    return pl.pallas_call(
        functools.partial(_in_proj_kernel, tw=tw, q_scale=HEAD_DIM ** -0.5 * LOG2E, n_alias=n_alias),
        grid=(n // tm, nj),
        in_specs=in_specs,
        out_specs=(tile, stacked, tile, stacked, tile, tile,
                   pl.BlockSpec((tm, LANES), lambda i, j: (i, 0))),
        out_shape=out_shape,
        scratch_shapes=[pltpu.VMEM((tm, d), BF16)],
        input_output_aliases=aliases,
        compiler_params=_params(("parallel", "arbitrary")),
        name="in_proj",
    )(*args)


def _cumsum_kernel(lf_ref, init_ref, c_ref, *, tb):
    nh, t = lf_ref.shape[1], lf_ref.shape[2]
    row = lax.broadcasted_iota(jnp.int32, (tb, tb), 0)
    col = lax.broadcasted_iota(jnp.int32, (tb, tb), 1)
    tri = jnp.where(row <= col, 1.0, 0.0).astype(BF16)
    carry = init_ref[0, :, 0:1]
    for j in range(t // tb):
        x = lf_ref[0, :, j * tb:(j + 1) * tb]
        x1 = x.astype(BF16)
        r1 = x - x1.astype(F32)
        x2 = r1.astype(BF16)
        x3 = (r1 - x2.astype(F32)).astype(BF16)
        cs = _dot(x1, tri) + _dot(x2, tri) + _dot(x3, tri) + carry
        c_ref[0, :, j * tb:(j + 1) * tb] = cs
        carry = cs[:, tb - 1:tb]


def _cumsum(lf, init):
    b, h, t = lf.shape
    return pl.pallas_call(
        functools.partial(_cumsum_kernel, tb=_tile(t, 256, LANES)),
        grid=(b,),
        in_specs=[pl.BlockSpec((1, h, t), lambda i: (i, 0, 0)),
                  pl.BlockSpec((1, h, LANES), lambda i: (i, 0, 0))],
        out_specs=pl.BlockSpec((1, h, t), lambda i: (i, 0, 0)),
        out_shape=jax.ShapeDtypeStruct((b, h, t), F32),
        compiler_params=_params(("parallel",)),
        name="cumsum",
    )(lf, init)


def _attn_kernel(q_ref, k_ref, v_ref, ck_ref, o_ref, *, tq):
    t = q_ref.shape[1]
    ck = ck_ref[0] * LOG2E
    row = lax.broadcasted_iota(jnp.int32, (tq, tq), 0)
    col = lax.broadcasted_iota(jnp.int32, (tq, tq), 1)
    for qi in range(t // tq):
        lo = qi * tq
        q = q_ref[0, lo:lo + tq, :]
        s_d = _dot_nt(q, k_ref[0, lo:lo + tq, :]) - ck[:, lo:lo + tq]
        s_d = jnp.where(col <= row, s_d, NEG_INF)
        m = jnp.max(s_d, axis=-1, keepdims=True)
        if qi > 0:
            s_o = _dot_nt(q, k_ref[0, :lo, :]) - ck[:, :lo]
            m = jnp.maximum(m, jnp.max(s_o, axis=-1, keepdims=True))
            p_o = jnp.exp2(s_o - m)
            l = jnp.sum(p_o, axis=-1, keepdims=True)
            acc = _dot(p_o.astype(BF16), v_ref[0, :lo, :])
        p_d = jnp.exp2(s_d - m)
        l_d = jnp.sum(p_d, axis=-1, keepdims=True)
        acc_d = _dot(p_d.astype(BF16), v_ref[0, lo:lo + tq, :])
        if qi > 0:
            l_d = l_d + l
            acc_d = acc_d + acc
        o_ref[0, lo:lo + tq, :] = acc_d / l_d


def _attention(q, k, v, ck, *, tq):
    b, t, width = q.shape
    nh = width // HEAD_DIM
    seq = pl.BlockSpec((1, t, HEAD_DIM), lambda bi, h: (bi, 0, h))
    return pl.pallas_call(
        functools.partial(_attn_kernel, tq=tq),
        grid=(b, nh),
        in_specs=[seq, seq, seq, pl.BlockSpec((1, 1, t), lambda bi, h: (bi * nh + h, 0, 0))],
        out_specs=seq,
        out_shape=jax.ShapeDtypeStruct((b, t, width), F32),
        compiler_params=_params(("parallel", "parallel")),
        name="attention",
    )(q, k, v, ck)


def _online_softmax_step(s, v, m_ref, l_ref, acc_ref, sl):
    m_prev = m_ref[:, sl]
    m_new = jnp.maximum(m_prev, jnp.max(s, axis=-1, keepdims=True))
    alpha = jnp.exp2(m_prev - m_new)
    p = jnp.exp2(s - m_new[:, 0:1])
    l_ref[:, sl] = alpha * l_ref[:, sl] + jnp.sum(p, axis=-1, keepdims=True)
    acc_ref[:, sl] = alpha * acc_ref[:, sl] + _dot(p.astype(BF16), v)
    m_ref[:, sl] = m_new


def _attn_cached_kernel(q_ref, kc_ref, vc_ref, ckp_ref, kn_ref, vn_ref, ckn_ref,
                        o_ref, m_ref, l_ref, acc_ref, *, nh, nkv):
    j = pl.program_id(1)

    @pl.when(j == 0)
    def _():
        m_ref[...] = jnp.full(m_ref.shape, NEG_INF, F32)
        l_ref[...] = jnp.zeros(l_ref.shape, F32)
        acc_ref[...] = jnp.zeros(acc_ref.shape, F32)

    for h in range(nh):
        sl = slice(h * HEAD_DIM, (h + 1) * HEAD_DIM)
        kh = kc_ref[:, h, :].astype(BF16)
        vh = vc_ref[:, h, :].astype(BF16)
        s = _dot_nt(q_ref[0, :, sl], kh) - ckp_ref[0, h:h + 1, :] * LOG2E
        _online_softmax_step(s, vh, m_ref, l_ref, acc_ref, sl)

    @pl.when(j == nkv - 1)
    def _():
        for h in range(nh):
            sl = slice(h * HEAD_DIM, (h + 1) * HEAD_DIM)
            s = _dot_nt(q_ref[0, :, sl], kn_ref[0, :, sl]) - ckn_ref[0, h:h + 1, :] * LOG2E
            row = lax.broadcasted_iota(jnp.int32, s.shape, 0)
            col = lax.broadcasted_iota(jnp.int32, s.shape, 1)
            s = jnp.where(col <= row, s, NEG_INF)
            _online_softmax_step(s, vn_ref[0, :, sl], m_ref, l_ref, acc_ref, sl)
        o_ref[0] = acc_ref[...] / l_ref[...]


def _attention_cached(l, q, k_cache, v_cache, ck_past, k_new, v_new, ck_new, *, tk):
    b, t, width = q.shape
    nh = width // HEAD_DIM
    past = k_cache.shape[2]
    nkv = past // tk
    per_b = lambda shape: pl.BlockSpec(shape, lambda i, j: (i, 0, 0))
    cache_spec = pl.BlockSpec((None, None, tk, nh, HEAD_DIM), lambda i, j: (l, i, j, 0, 0))
    return pl.pallas_call(
        functools.partial(_attn_cached_kernel, nh=nh, nkv=nkv),
        grid=(b, nkv),
        in_specs=[
            per_b((1, t, width)),
            cache_spec,
            cache_spec,
            pl.BlockSpec((1, nh, tk), lambda i, j: (i, 0, j)),
            per_b((1, t, width)),
            per_b((1, t, width)),
            per_b((1, nh, t)),
        ],
        out_specs=per_b((1, t, width)),
        out_shape=jax.ShapeDtypeStruct((b, t, width), F32),
        scratch_shapes=[pltpu.VMEM((t, width), F32)] * 3,
        compiler_params=_params(("parallel", "arbitrary")),
        name="attention_cached",
    )(q, k_cache, v_cache, ck_past, k_new, v_new, ck_new)


def _on_block_diagonal(x, g, width):
    reps = LANES // x.shape[1]
    x128 = jnp.concatenate([x] * reps, axis=1) if reps > 1 else x
    wide = jnp.concatenate([x128] * (GROUPS_PER_BLOCK * x.shape[1] // LANES), axis=1)
    lane = lax.broadcasted_iota(jnp.int32, wide.shape, 1)
    lo = g * x.shape[1]
    return jnp.where((lane >= lo) & (lane < lo + x.shape[1]), wide, 0.0)


def _s5_prep_kernel(are_ref, aim_ref, ldt_ref, bre_ref, bim_ref, cre_ref, cim_ref,
                    r_ref, wst_ref, wcar_ref, a16_ref):
    L, C, P = S5_CHUNK, SSM_GROUP, SSM_STATE
    tau = lax.broadcasted_iota(jnp.int32, (L + 1, 1, 1), 0).astype(F32)
    hp = lax.Precision.HIGHEST
    a16_re, a16_im = [], []
    for g in range(GROUPS_PER_BLOCK):
        ar = are_ref[0, g:g + 1, :]
        ai = aim_ref[0, g:g + 1, :]
        dt = jnp.exp(ldt_ref[0, g:g + 1, :])
        mag = jnp.exp(tau * (ar * dt)[None])
        pw_re = mag * jnp.cos(tau * (ai * dt)[None])
        pw_im = mag * jnp.sin(tau * (ai * dt)[None])
        a16_re.append(pw_re[L])
        a16_im.append(pw_im[L])
        nr = pw_re[1] - 1.0
        ni = pw_im[1]
        den = ar * ar + ai * ai
        coef_re = (nr * ar + ni * ai) / den
        coef_im = (ni * ar - nr * ai) / den
        b_re = bre_ref[0, g]
        b_im = bim_ref[0, g]
        bb_re = coef_re * b_re - coef_im * b_im
        bb_im = coef_re * b_im + coef_im * b_re
        bp_re = pw_re * bb_re[None] - pw_im * bb_im[None]
        bp_im = pw_re * bb_im[None] + pw_im * bb_re[None]
        c_re = cre_ref[0, g][None]
        c_im = cim_ref[0, g][None]
        cp_re = c_re * pw_re - c_im * pw_im
        cp_im = c_re * pw_im + c_im * pw_re
        kt = (lax.dot_general(bp_re[:L].reshape(L * C, P), cre_ref[0, g], (((1,), (1,)), ((), ())),
                              precision=hp, preferred_element_type=F32)
              - lax.dot_general(bp_im[:L].reshape(L * C, P), cim_ref[0, g], (((1,), (1,)), ((), ())),
                                precision=hp, preferred_element_type=F32))
        for s in range(L):
            rows = slice(s * LANES + g * C, s * LANES + (g + 1) * C)
            wst_ref[0, rows, :STATE_HALF] = _on_block_diagonal(bp_re[L - 1 - s], g, P).astype(BF16)
            wst_ref[0, rows, STATE_HALF:] = _on_block_diagonal(bp_im[L - 1 - s], g, P).astype(BF16)
            wcar_ref[0, rows, :STATE_HALF] = _on_block_diagonal(cp_re[s + 1], g, P).astype(BF16)
            wcar_ref[0, rows, STATE_HALF:] = _on_block_diagonal(-cp_im[s + 1], g, P).astype(BF16)
            tau_s = L - 1 - s
            r_ref[0, rows, :] = _on_block_diagonal(kt[tau_s * C:(tau_s + 1) * C, :], g, C).astype(BF16)
    half = STATE_TILES // 2
    for j in range(half):
        a16_ref[0, j] = jnp.concatenate(a16_re[2 * j:2 * j + 2], axis=1)
        a16_ref[0, half + j] = jnp.concatenate(a16_im[2 * j:2 * j + 2], axis=1)


def _s5_prep(a_re, a_im, log_dt, b_re, b_im, c_re, c_im):
    nl, ng, ns = a_re.shape
    gpb = GROUPS_PER_BLOCK
    n = nl * ng // gpb
    blk = lambda a: a.reshape((n, gpb) + a.shape[2:])
    args = (blk(a_re), blk(a_im), blk(log_dt[..., None]),
            blk(jnp.swapaxes(b_re, -1, -2)), blk(jnp.swapaxes(b_im, -1, -2)),
            blk(c_re), blk(c_im))
    spec = lambda shape: pl.BlockSpec((1,) + shape, lambda i: (i,) + (0,) * len(shape))
    c, p = SSM_GROUP, SSM_STATE
    out_dims = [(CHUNK_W, LANES), (CHUNK_W, STATE_W), (CHUNK_W, STATE_W), (STATE_TILES, 1, LANES)]
    out_dtypes = [BF16, BF16, BF16, F32]
    r, wst, wcar, a16 = pl.pallas_call(
        _s5_prep_kernel,
        grid=(n,),
        in_specs=[spec((gpb, p)), spec((gpb, p)), spec((gpb, 1)),
                  spec((gpb, c, p)), spec((gpb, c, p)), spec((gpb, c, p)), spec((gpb, c, p))],
        out_specs=tuple(spec(d) for d in out_dims),
        out_shape=tuple(jax.ShapeDtypeStruct((n,) + d, t) for d, t in zip(out_dims, out_dtypes)),
        compiler_params=_params(("parallel",)),
        name="s5_prep",
    )(*args)
    return r, wst, wcar, a16.reshape(n * STATE_TILES, 1, LANES)


def _gather_chunk_inputs(u_ref, uflat_ref, tr):
    for s in range(S5_CHUNK):
        uflat_ref[:, s * LANES:(s + 1) * LANES] = (
            u_ref[pl.ds(s, tr, stride=S5_CHUNK), :].astype(BF16))


def _s5_state_kernel(u_ref, wst_ref, s_ref, uflat_ref, *, tr):
    _gather_chunk_inputs(u_ref, uflat_ref, tr)
    s = _dot(uflat_ref[...], wst_ref[...])
    for j in range(STATE_TILES):
        s_ref[j] = s[:, j * LANES:(j + 1) * LANES]


def _s5_state(u, wst, blk0, *, tr):
    n, width = u.shape
    rows = n // S5_CHUNK
    nblk = width // LANES
    return pl.pallas_call(
        functools.partial(_s5_state_kernel, tr=tr),
        grid=(nblk, rows // tr),
        in_specs=[pl.BlockSpec((tr * S5_CHUNK, LANES), lambda g, i: (i, g)),
                  pl.BlockSpec((None, CHUNK_W, STATE_W), lambda g, i: (blk0 + g, 0, 0))],
        out_specs=pl.BlockSpec((STATE_TILES, tr, LANES), lambda g, i: (g, i, 0)),
        out_shape=jax.ShapeDtypeStruct((nblk * STATE_TILES, rows, LANES), F32),
        scratch_shapes=[pltpu.VMEM((tr, CHUNK_W), BF16)],
        compiler_params=_params(("parallel", "parallel")),
        name="s5_state",
    )(u, wst)


def _s5_scan_kernel(s_ref, a_ref, x0_ref, xp_ref, xf_ref, *, nb, nck):
    half = STATE_TILES // 2
    a = [a_ref[j] for j in range(STATE_TILES)]

    def body(k, x):
        rows = pl.ds(k, nb, stride=nck)
        new = [None] * STATE_TILES
        for j in range(half):
            xp_ref.at[j][rows, :] = x[j]
            xp_ref.at[half + j][rows, :] = x[half + j]
            new[j] = a[j] * x[j] - a[half + j] * x[half + j] + s_ref.at[j][rows, :]
            new[half + j] = a[j] * x[half + j] + a[half + j] * x[j] + s_ref.at[half + j][rows, :]
        return tuple(new)

    x = lax.fori_loop(0, nck, body, tuple(x0_ref[j] for j in range(STATE_TILES)))
    for j in range(STATE_TILES):
        xf_ref[j] = x[j]


def _s5_scan(s, a16, x0, blk0, *, nb):
    tiles, rows, _ = s.shape
    nblk = tiles // STATE_TILES
    nck = rows // nb
    blk = lambda r: pl.BlockSpec((STATE_TILES, r, LANES), lambda g: (g, 0, 0))
    return pl.pallas_call(
        functools.partial(_s5_scan_kernel, nb=nb, nck=nck),
        grid=(nblk,),
        in_specs=[blk(rows),
                  pl.BlockSpec((STATE_TILES, 1, LANES), lambda g: (blk0 + g, 0, 0)),
                  blk(nb)],
        out_specs=(blk(rows), blk(nb)),
        out_shape=(jax.ShapeDtypeStruct((tiles, rows, LANES), F32),
                   jax.ShapeDtypeStruct((tiles, nb, LANES), F32)),
        compiler_params=_params(("parallel",)),
        name="s5_scan",
    )(s, a16, x0)


def _s5_out_kernel(u_ref, xp_ref, r_ref, wcar_ref, d_ref, g_ref, uflat_ref, *, tr):
    _gather_chunk_inputs(u_ref, uflat_ref, tr)
    x_prev = jnp.concatenate([xp_ref[j].astype(BF16) for j in range(STATE_TILES)], axis=1)
    carried = _dot_nt(x_prev, wcar_ref[...])
    d = d_ref[...]
    for t in range(S5_CHUNK):
        rows = pl.ds(t, tr, stride=S5_CHUNK)
        y = (carried[:, t * LANES:(t + 1) * LANES]
             + _dot(uflat_ref[:, :(t + 1) * LANES], r_ref[(S5_CHUNK - 1 - t) * LANES:, :])
             + d * u_ref[rows, :])
        g_ref[rows, :] = _gelu_tanh(y)


def _s5_out(l, u, x_prev, r, wcar, d_skip, blk0, *, tr):
    n, width = u.shape
    rows = n // S5_CHUNK
    nblk = width // LANES
    return pl.pallas_call(
        functools.partial(_s5_out_kernel, tr=tr),
        grid=(nblk, rows // tr),
        in_specs=[pl.BlockSpec((tr * S5_CHUNK, LANES), lambda g, i: (i, g)),
                  pl.BlockSpec((STATE_TILES, tr, LANES), lambda g, i: (g, i, 0)),
                  pl.BlockSpec((None, CHUNK_W, LANES), lambda g, i: (blk0 + g, 0, 0)),
                  pl.BlockSpec((None, CHUNK_W, STATE_W), lambda g, i: (blk0 + g, 0, 0)),
                  pl.BlockSpec((None, 1, LANES), lambda g, i: (l, 0, g))],
        out_specs=pl.BlockSpec((tr * S5_CHUNK, LANES), lambda g, i: (i, g)),
        out_shape=jax.ShapeDtypeStruct((n, width), F32),
        scratch_shapes=[pltpu.VMEM((tr, CHUNK_W), BF16)],
        compiler_params=_params(("parallel", "parallel")),
        name="s5_out",
    )(u, x_prev, r, wcar, d_skip)


def _mix_out_kernel(attn_ref, g_ref, x_ref, wglu_ref, bglu_ref, gat_ref, gss_ref,
                    wo_a_ref, wo_s_ref, h_ref):
    g = g_ref[...]
    gate = _sigmoid(_dot(g.astype(BF16), wglu_ref[...]) + bglu_ref[...])
    ssm_n = _rms(g * gate, gss_ref[...]).astype(BF16)
    attn_n = _rms(attn_ref[...], gat_ref[...]).astype(BF16)
    h_ref[...] = x_ref[...] + _dot(attn_n, wo_a_ref[...]) + _dot(ssm_n, wo_s_ref[...])


def _mix_out(l, attn, g, x, w, *, tm):
    n, d = x.shape
    wa = attn.shape[1]
    ws = g.shape[1]
    assert wa == ws
    return pl.pallas_call(
        _mix_out_kernel,
        grid=(n // tm,),
        in_specs=[pl.BlockSpec((tm, wa), lambda i: (i, 0)),
                  pl.BlockSpec((tm, ws), lambda i: (i, 0)),
                  pl.BlockSpec((tm, d), lambda i: (i, 0)),
                  pl.BlockSpec((None, ws, ws), lambda i: (l, 0, 0)),
                  _layer_vec(ws)(l), _layer_vec(wa)(l), _layer_vec(ws)(l),
                  pl.BlockSpec((None, wa, d), lambda i: (l, 0, 0)),
                  pl.BlockSpec((None, ws, d), lambda i: (l, 1, 0))],
        out_specs=pl.BlockSpec((tm, d), lambda i: (i, 0)),
        out_shape=jax.ShapeDtypeStruct((n, d), F32),
        compiler_params=_params(("parallel",)),
        name="mix_out",
    )(attn, g, x, w["w_glu"], w["b_glu"], w["g_attn"], w["g_ssm"], w["w_out"], w["w_out"])


def _ffn_kernel(h_ref, g_ref, wg_ref, wu_ref, wd_ref, o_ref, fn_ref, acc_ref, *, nf):
    f = pl.program_id(1)

    @pl.when(f == 0)
    def _():
        fn_ref[...] = _rms(h_ref[...], g_ref[...]).astype(BF16)
        acc_ref[...] = h_ref[...]

    fn = fn_ref[...]
    gate = _dot(fn, wg_ref[...])
    up = _dot(fn, wu_ref[...])
    act = (gate * _sigmoid(gate) * up).astype(BF16)
    acc_ref[...] += _dot(act, wd_ref[...])

    @pl.when(f == nf - 1)
    def _():
        o_ref[...] = acc_ref[...]


def _ffn(l, h, w, *, tm, tf):
    n, d = h.shape
    dff = w["w_gate"].shape[2]
    nf = dff // tf
    return pl.pallas_call(
        functools.partial(_ffn_kernel, nf=nf),
        grid=(n // tm, nf),
        in_specs=[pl.BlockSpec((tm, d), lambda i, f: (i, 0)),
                  _layer_vec(d)(l),
                  pl.BlockSpec((None, d, tf), lambda i, f: (l, 0, f)),
                  pl.BlockSpec((None, d, tf), lambda i, f: (l, 0, f)),
                  pl.BlockSpec((None, tf, d), lambda i, f: (l, f, 0))],
        out_specs=pl.BlockSpec((tm, d), lambda i, f: (i, 0)),
        out_shape=jax.ShapeDtypeStruct((n, d), F32),
        scratch_shapes=[pltpu.VMEM((tm, d), BF16), pltpu.VMEM((tm, d), F32)],
        compiler_params=_params(("parallel", "arbitrary")),
        name="ffn",
    )(h, w["g_ffn"], w["w_gate"], w["w_up"], w["w_down"])


def _ple_kernel(h_ref, p_ref, g_ref, wg_ref, wp_ref, o_ref):
    h = h_ref[...]
    gate = _sigmoid(_dot(_rms(h, g_ref[...]).astype(BF16), wg_ref[...]))
    o_ref[...] = h + gate * _dot(p_ref[...].astype(BF16), wp_ref[...])


def _ple(l, h, p, w, *, tm):
    n, d = h.shape
    dp = p.shape[1]
    return pl.pallas_call(
        _ple_kernel,
        grid=(n // tm,),
        in_specs=[pl.BlockSpec((tm, d), lambda i: (i, 0)),
                  pl.BlockSpec((tm, dp), lambda i: (i, 0)),
                  _layer_vec(d)(l),
                  pl.BlockSpec((None, d, d), lambda i: (l, 0, 0)),
                  pl.BlockSpec((None, dp, d), lambda i: (l, 0, 0))],
        out_specs=pl.BlockSpec((tm, d), lambda i: (i, 0)),
        out_shape=jax.ShapeDtypeStruct((n, d), F32),
        compiler_params=_params(("parallel",)),
        name="ple",
    )(h, p, w["g_ple"], w["w_ple_gate"], w["w_ple_proj"])


def _layer(l, x, p_l, cache, state0, w, s5, nh, kv_all):
    b, t, d = x.shape
    n = b * t
    wa = nh * HEAD_DIM
    xf = x.reshape(n, d)
    q, k_all, kb, v_all, vb, u, lf = _in_proj(l, xf, w, kv_all, tm=_tile(n, 1024))
    logf = lf[:, :nh].reshape(b, t, nh)
    lf_rows = jnp.swapaxes(logf, 1, 2)
    zero_init = jnp.zeros((b, nh, LANES), F32)
    seq = lambda a: a.reshape(b, t, wa)
    if cache is None:
        ck = _cumsum(lf_rows, zero_init)
        attn = _attention(seq(q), seq(kb), seq(vb), ck.reshape(b * nh, 1, t), tq=_tile(t, 512))
    else:
        k_past, v_past, lf_past = cache
        nl, _, past = k_past.shape[:3]
        ck_past = _cumsum(jnp.swapaxes(lf_past, 1, 2), zero_init)
        init = jnp.broadcast_to(ck_past[:, :, past - 1:past], (b, nh, LANES))
        ck_new = _cumsum(jnp.pad(lf_rows, ((0, 0), (0, 0), (0, LANES - t))), init)
        attn = _attention_cached(l, seq(q), k_past, v_past, ck_past, seq(kb), seq(vb),
                                 ck_new[:, :, :t], tk=_tile(past, 2048))
    attn = attn.reshape(n, wa)

    r, wst, wcar, a16 = s5
    nblk = u.shape[1] // LANES
    rows = n // S5_CHUNK
    tr = _tile(rows, 256)
    s_contrib = _s5_state(u, wst, l * nblk, tr=tr)
    x_prev, x_fin = _s5_scan(s_contrib, a16, state0, l * nblk, nb=b)
    g = _s5_out(l, u, x_prev, r, wcar, w["d_skip"], l * nblk, tr=tr)

    h = _mix_out(l, attn, g, xf, w, tm=_tile(n, 256))
    h = _ffn(l, h, w, tm=_tile(n, 512), tf=_tile(w["w_gate"].shape[2], 512, LANES))
    h = _ple(l, h, p_l.reshape(n, -1), w, tm=_tile(n, 512))

    xs = x_fin.reshape(nblk, 2, STATE_TILES // 2, b, LANES).transpose(3, 0, 1, 2, 4)
    xs = xs.reshape(b, nblk, 2, GROUPS_PER_BLOCK, SSM_STATE)
    s_re = xs[:, :, 0].reshape(b, nblk * GROUPS_PER_BLOCK, SSM_STATE)
    s_im = xs[:, :, 1].reshape(b, nblk * GROUPS_PER_BLOCK, SSM_STATE)
    return h.reshape(b, t, d), (k_all, v_all), logf, s_re, s_im


def _pack_state(s_re, s_im):
    b, g, p = s_re.shape
    nblk = g // GROUPS_PER_BLOCK
    half = STATE_TILES // 2
    both = jnp.stack([s_re.reshape(b, nblk, half, LANES), s_im.reshape(b, nblk, half, LANES)], axis=2)
    return both.transpose(1, 2, 3, 0, 4).reshape(nblk * STATE_TILES, b, LANES)


def kernel(x_prompt, x_sample, cache_k, cache_v, cache_logf, state_ssm_re, state_ssm_im, p_prompt, p_sample, g_mix, w_in, b_f, g_q, g_k, a_re, a_im, log_dt, b_re, b_im, c_re, c_im, d_skip, w_glu, b_glu, g_attn_out, g_ssm_out, w_out, g_ffn, w_gate, w_up, w_down, g_ple, w_ple_gate, w_ple_proj):
    depth = w_in.shape[0]
    nh = b_f.shape[1]
    wa = nh * HEAD_DIM
    ws = d_skip.shape[1]
    assert wa == ws, "q/k/v/u column tiles are walked together assuming equal group widths"
    nblk = ws // LANES
    vec = lambda a: a[:, None, :]
    w = dict(
        g_mix=vec(g_mix), g_q=vec(g_q), g_k=vec(g_k), d_skip=vec(d_skip), b_glu=vec(b_glu),
        g_attn=vec(g_attn_out), g_ssm=vec(g_ssm_out), g_ffn=vec(g_ffn), g_ple=vec(g_ple),
        b_f=vec(jnp.pad(b_f, ((0, 0), (0, LANES - nh)))),
        w_qkv=_cast_bf16(w_in, cols=3 * wa),
        w_u=_cast_bf16(w_in[:, :, 3 * wa + nh:]),
        w_f=jnp.pad(w_in[:, :, 3 * wa:3 * wa + nh], ((0, 0), (0, 0), (0, LANES - nh))).astype(BF16),
        w_glu=_cast_bf16(w_glu), w_out=_cast_bf16(w_out), w_gate=_cast_bf16(w_gate),
        w_up=_cast_bf16(w_up), w_down=_cast_bf16(w_down), w_ple_gate=_cast_bf16(w_ple_gate),
        w_ple_proj=_cast_bf16(w_ple_proj))
    s5 = _s5_prep(a_re, a_im, log_dt, b_re, b_im, c_re, c_im)
    zero_state = jnp.zeros((nblk * STATE_TILES, x_prompt.shape[0], LANES), F32)
    hp, hs = x_prompt, x_sample
    kv_p = kv_s = None
    lf_p, re_p, im_p, lf_s, re_s, im_s = [], [], [], [], [], []
    for l in range(depth):
        hp, kv_p, lf, s_re, s_im = _layer(l, hp, p_prompt[l], None, zero_state, w, s5, nh, kv_p)
        lf_p.append(lf); re_p.append(s_re); im_p.append(s_im)
        hs, kv_s, lf, s_re, s_im = _layer(l, hs, p_sample[l], (cache_k, cache_v, cache_logf[l]),
                                          _pack_state(state_ssm_re[l], state_ssm_im[l]), w, s5, nh, kv_s)
        lf_s.append(lf); re_s.append(s_re); im_s.append(s_im)
    bp, tp = x_prompt.shape[:2]
    bs, ts = x_sample.shape[:2]
    kv5 = lambda a, b, t: a.reshape(depth, b, t, nh, HEAD_DIM)
    return (hp, hs,
            kv5(kv_p[0], bp, tp), kv5(kv_p[1], bp, tp), jnp.stack(lf_p), jnp.stack(re_p), jnp.stack(im_p),
            kv5(kv_s[0], bs, ts), kv5(kv_s[1], bs, ts), jnp.stack(lf_s), jnp.stack(re_s), jnp.stack(im_s))
```
